```python
import math
import jax, jax.numpy as jnp
from jax import lax
import numpy as np

D_MODEL = 2048
BATCH = 16
SEQ = 256
DEPTH = 2
DEC_BATCH = 2
DEC_SEQ = 4096
PAST_LEN = 512

GRID_W = 64
A_HEADS = 8
A_DH = 64
A_DV = 2 * A_DH
A_QK_W = A_HEADS * 2 * A_DH
A_WIDTH = A_HEADS * A_DV
ROPE_FREQS = A_DH // 4
ROPE_BASE = 10000.0
Q_BLOCK = 128
R_HEADS = 8
R_DK = 128
R_DV = 128
R_QK_W = R_HEADS * R_DK
R_WIDTH = R_HEADS * R_DV
R_CHUNK = 128
C_WIDTH = 1024
C_KERNEL = 31
N_BRANCH = 3
BRANCH_W = 1024
N_IN = 2 * A_QK_W + A_WIDTH + 2 * R_QK_W + 3 * R_WIDTH + 2 * C_WIDTH + N_BRANCH * D_MODEL
N_GROUPS = 4
EXP_PER_GROUP = 4
N_EXPERTS = N_GROUPS * EXP_PER_GROUP
TOPK_IN_GROUP = 2
E_FF = 512
EPS = 1e-6

kernel_name = "hybrid_diff_ret_conv_hmoe_prefix_step"


def _in_split_points():
    sizes = (A_QK_W, A_QK_W, A_WIDTH, R_QK_W, R_QK_W, R_WIDTH, R_WIDTH, R_WIDTH,
             2 * C_WIDTH, N_BRANCH * D_MODEL)
    pts, acc = [], 0
    for s in sizes[:-1]:
        acc += s
        pts.append(acc)
    return pts


def rms_norm(x, g):
    xf = x.astype(jnp.float32)
    y = xf * lax.rsqrt(jnp.mean(xf * xf, axis=-1, keepdims=True) + EPS)
    return (y * g.astype(jnp.float32)).astype(x.dtype)


def layer_norm(x, g, b):
    xf = x.astype(jnp.float32)
    mu = jnp.mean(xf, axis=-1, keepdims=True)
    var = jnp.mean(jnp.square(xf - mu), axis=-1, keepdims=True)
    y = (xf - mu) * lax.rsqrt(var + EPS) * g.astype(jnp.float32) + b.astype(jnp.float32)
    return y.astype(x.dtype)


def head_group_norm(o):
    mu = jnp.mean(o, axis=-1, keepdims=True)
    var = jnp.mean(jnp.square(o - mu), axis=-1, keepdims=True)
    return (o - mu) * lax.rsqrt(var + EPS)


def ada_params(cond, w_mod, b_mod):
    m = jax.nn.silu(cond) @ w_mod + b_mod
    return jnp.split(m[:, None, :], 6, axis=-1)


def modulate(h, shift, scale):
    return h * (1 + scale) + shift


def axial_rope_tables(n_tokens):
    n_rows = n_tokens // GRID_W
    row = jnp.repeat(jnp.arange(n_rows), GRID_W).astype(jnp.float32)
    col = jnp.tile(jnp.arange(GRID_W), n_rows).astype(jnp.float32)
    inv = ROPE_BASE ** (-jnp.arange(ROPE_FREQS, dtype=jnp.float32) / ROPE_FREQS)
    ang = jnp.stack([row[:, None] * inv, col[:, None] * inv], axis=1)
    return jnp.cos(ang), jnp.sin(ang)


def apply_axial_rope(x, cos, sin):
    shp = x.shape
    xa = x.astype(jnp.float32).reshape(shp[:-1] + (2, 2, ROPE_FREQS))
    x1, x2 = xa[..., 0, :], xa[..., 1, :]
    c = cos[None, :, None, None]
    s = sin[None, :, None, None]
    out = jnp.stack([x1 * c - x2 * s, x1 * s + x2 * c], axis=-2)
    return out.reshape(shp).astype(x.dtype)


def diff_attention(q, k, v, lam):
    b, tq, h = q.shape[:3]
    nblk = tq // Q_BLOCK
    qb = jnp.moveaxis(q.reshape((b, nblk, Q_BLOCK) + q.shape[2:]), 1, 0)
    scale = A_DH ** -0.5

    def block(qblk):
        s = jnp.einsum('bqhmd,bkhmd->bmhqk', qblk, k).astype(jnp.float32) * scale
        p = jax.nn.softmax(s, axis=-1)
        a = p[:, 0] - lam * p[:, 1]
        return jnp.einsum('bhqk,bkhe->bqhe', a.astype(v.dtype), v)

    o = lax.map(block, qb)
    return jnp.moveaxis(o, 0, 1).reshape(b, tq, h, A_DV)


def retention_scan(q, k, v, log_gamma, s0):
    b, t, h, _ = q.shape
    n = t // R_CHUNK
    idx = jnp.arange(R_CHUNK, dtype=jnp.float32)
    diff = idx[:, None] - idx[None, :]
    d_in = jnp.where(diff >= 0, jnp.exp(jnp.maximum(diff, 0.0)[None] * log_gamma[:, None, None]), 0.0)
    q_dec = jnp.exp((idx + 1)[:, None] * log_gamma[None, :])
    k_dec = jnp.exp((R_CHUNK - 1 - idx)[:, None] * log_gamma[None, :])
    c_dec = jnp.exp(R_CHUNK * log_gamma)

    def chunks(a):
        return jnp.moveaxis(a.reshape((b, n, R_CHUNK) + a.shape[2:]), 1, 0)

    def step(s, inp):
        qc, kc, vc = inp
        att = jnp.einsum('bihd,bjhd->bhij', qc, kc) * d_in
        o = (jnp.einsum('bhij,bjhe->bihe', att, vc)
             + jnp.einsum('bihd,bhde->bihe', qc, s) * q_dec[None, :, :, None])
        s = s * c_dec[None, :, None, None] + jnp.einsum('bjhd,bjhe->bhde', kc * k_dec[None, :, :, None], vc)
        return s, o

    s_fin, o = lax.scan(step, s0, (chunks(q), chunks(k), chunks(v)))
    return jnp.moveaxis(o, 0, 1).reshape(b, t, h, v.shape[-1]), s_fin


def bidir_retention(rq, rk, rv, rgf, rgb, decay_logit, s0f, s0b):
    b, t, _ = rq.shape
    q = rq.astype(jnp.float32).reshape(b, t, R_HEADS, R_DK)
    k = rk.astype(jnp.float32).reshape(b, t, R_HEADS, R_DK) * (R_DK ** -0.5)
    v = rv.astype(jnp.float32).reshape(b, t, R_HEADS, R_DV)
    gf = rgf.astype(jnp.float32).reshape(b, t, R_HEADS, R_DV)
    gb = rgb.astype(jnp.float32).reshape(b, t, R_HEADS, R_DV)
    lg = jax.nn.log_sigmoid(decay_logit.astype(jnp.float32))
    of, sf = retention_scan(q, k, v, lg[0], s0f.astype(jnp.float32))
    ob_r, sb = retention_scan(jnp.flip(q, 1), jnp.flip(k, 1), jnp.flip(v, 1), lg[1], s0b.astype(jnp.float32))
    ob = jnp.flip(ob_r, 1)
    y = head_group_norm(of) * jax.nn.silu(gf) + head_group_norm(ob) * jax.nn.silu(gb)
    return y.reshape(b, t, R_WIDTH).astype(rq.dtype), sf, sb


def conformer_conv(u2, w_dw, b_dw, ln_g, ln_b):
    a, g = jnp.split(u2, 2, axis=-1)
    u = a * jax.nn.sigmoid(g)
    pad = (C_KERNEL - 1) // 2
    y = lax.conv_general_dilated(u, w_dw[:, None, :].astype(u.dtype), (1,), [(pad, pad)],
                                 dimension_numbers=('NWC', 'WIO', 'NWC'),
                                 feature_group_count=C_WIDTH) + b_dw
    return jax.nn.silu(layer_norm(y, ln_g, ln_b))


def hier_moe(h, w_grp, b_grp, w_rt, b_rt, w_gate, w_up, w_down):
    b, t, _ = h.shape
    g_logit = (h @ w_grp + b_grp).astype(jnp.float32)
    g_prob = jax.nn.softmax(g_logit, axis=-1)
    g_sel = jnp.argmax(g_logit, axis=-1)
    g_w = jnp.take_along_axis(g_prob, g_sel[..., None], axis=-1)
    e_logit = (h @ w_rt + b_rt).astype(jnp.float32).reshape(b, t, N_GROUPS, EXP_PER_GROUP)
    e_logit = jnp.take_along_axis(e_logit, g_sel[..., None, None], axis=2)[:, :, 0]
    e_prob = jax.nn.softmax(e_logit, axis=-1)
    top_p, top_i = lax.top_k(e_prob, TOPK_IN_GROUP)
    top_p = top_p / jnp.sum(top_p, axis=-1, keepdims=True)
    eid = g_sel[..., None] * EXP_PER_GROUP + top_i
    combine = jnp.sum(jax.nn.one_hot(eid, N_EXPERTS, dtype=jnp.float32) * (g_w * top_p)[..., None], axis=-2)
    a = jnp.einsum('btd,edf->btef', h, w_gate)
    u = jnp.einsum('btd,edf->btef', h, w_up)
    act = jax.nn.silu(a) * u * combine[..., None].astype(h.dtype)
    return jnp.einsum('btef,efd->btd', act, w_down)


def trunk_layer(x, mods, lw, layer_idx, ctx=None):
    (w_in, g_n1, g_n2, g_q, g_k, lam_p, g_sub, decay, cw, cb, lng, lnb,
     w_br, w_o, w_grp, b_grp, w_rt, b_rt, w_gate, w_up, w_down) = lw
    shift1, scale1, gate1, shift2, scale2, gate2 = mods
    b, t, _ = x.shape
    h = modulate(rms_norm(x, g_n1), shift1, scale1)
    z = h @ w_in
    aq, ak, av, rq, rk, rv, rgf, rgb, cglu, glog = jnp.split(z, _in_split_points(), axis=-1)

    aq = rms_norm(aq.reshape(b, t, A_HEADS, 2, A_DH), g_q)
    ak = rms_norm(ak.reshape(b, t, A_HEADS, 2, A_DH), g_k)
    av = av.reshape(b, t, A_HEADS, A_DV)
    lam_init = 0.8 - 0.6 * math.exp(-0.3 * layer_idx)
    lp = lam_p.astype(jnp.float32)
    lam = jnp.exp(jnp.sum(lp[0] * lp[1])) - jnp.exp(jnp.sum(lp[2] * lp[3])) + lam_init
    if ctx is None:
        k_all, v_all = ak, av
        s0 = jnp.zeros((b, R_HEADS, R_DK, R_DV), jnp.float32)
        s0f, s0b = s0, s0
    else:
        ck, cv, cs = ctx
        cos, sin = axial_rope_tables(t)
        aq = apply_axial_rope(aq, cos, sin)
        ak = apply_axial_rope(ak, cos, sin)
        k_all = jnp.concatenate([ck.astype(ak.dtype), ak], axis=1)
        v_all = jnp.concatenate([cv.astype(av.dtype), av], axis=1)
        s0f, s0b = cs[:, 0], cs[:, 1]
    oa = diff_attention(aq, k_all, v_all, lam)
    oa = (rms_norm(oa, g_sub) * (1 - lam_init)).reshape(b, t, A_WIDTH).astype(x.dtype)

    orr, sf, sb = bidir_retention(rq, rk, rv, rgf, rgb, decay, s0f, s0b)

    oc = conformer_conv(cglu, cw, cb, lng, lnb)

    branches = jnp.stack([oa, orr.astype(x.dtype), oc.astype(x.dtype)], axis=2)
    proj = jnp.einsum('btnw,nwd->btnd', branches, w_br)
    gates = jax.nn.sigmoid(glog.reshape(b, t, N_BRANCH, D_MODEL))
    merged = jnp.sum(gates * proj, axis=2)
    x = x + gate1 * (merged @ w_o)

    h2 = modulate(rms_norm(x, g_n2), shift2, scale2)
    x = x + gate2 * hier_moe(h2, w_grp, b_grp, w_rt, b_rt, w_gate, w_up, w_down)
    return x, (ak, av, jnp.stack([sf, sb], axis=1))


def setup_inputs(seed: int = 0) -> dict:
    key = jax.random.key(seed)
    ks = jax.random.split(key, 32)
    f32 = jnp.float32

    def nrm(k, shape, scale):
        return jax.random.normal(k, shape, f32) * scale

    base = 1.0 - 2.0 ** (-5.0 - np.arange(R_HEADS))
    decay_logit0 = jnp.asarray(np.log(base / (1.0 - base)), dtype=f32)
    return {
        "x_prompt": nrm(ks[0], (BATCH, SEQ, D_MODEL), 1.0),
        "x_sample": nrm(ks[1], (DEC_BATCH, DEC_SEQ, D_MODEL), 1.0),
        "cache_diff_k": nrm(ks[2], (DEC_BATCH, DEPTH, PAST_LEN, A_HEADS, 2, A_DH), 1.0),
        "cache_diff_v": nrm(ks[3], (DEC_BATCH, DEPTH, PAST_LEN, A_HEADS, A_DV), 1.0),
        "state_ret": nrm(ks[4], (DEC_BATCH, DEPTH, 2, R_HEADS, R_DK, R_DV), 1.0),
        "c": nrm(ks[5], (DEC_BATCH, D_MODEL), 1.0),
        "c_ctx": nrm(ks[6], (D_MODEL,), 1.0),
        "w_mod": nrm(ks[7], (DEPTH, D_MODEL, 6 * D_MODEL), D_MODEL ** -0.5),
        "b_mod": nrm(ks[8], (DEPTH, 6 * D_MODEL), 0.02),
        "g_norm1": 1.0 + nrm(ks[9], (DEPTH, D_MODEL), 0.02),
        "g_norm2": 1.0 + nrm(ks[10], (DEPTH, D_MODEL), 0.02),
        "w_in": nrm(ks[11], (DEPTH, D_MODEL, N_IN), D_MODEL ** -0.5),
        "g_qnorm": 1.0 + nrm(ks[12], (DEPTH, A_DH), 0.02),
        "g_knorm": 1.0 + nrm(ks[13], (DEPTH, A_DH), 0.02),
        "lambda_qk": nrm(ks[14], (DEPTH, 4, A_DH), 0.1),
        "g_subln": 1.0 + nrm(ks[15], (DEPTH, A_DV), 0.02),
        "ret_decay_logit": decay_logit0[None, None, :] + nrm(ks[16], (DEPTH, 2, R_HEADS), 0.1),
        "conv_w": nrm(ks[17], (DEPTH, C_KERNEL, C_WIDTH), C_KERNEL ** -0.5),
        "conv_b": nrm(ks[18], (DEPTH, C_WIDTH), 0.02),
        "conv_ln_g": 1.0 + nrm(ks[19], (DEPTH, C_WIDTH), 0.02),
        "conv_ln_b": nrm(ks[20], (DEPTH, C_WIDTH), 0.02),
        "w_branch": nrm(ks[21], (DEPTH, N_BRANCH, BRANCH_W, D_MODEL), BRANCH_W ** -0.5),
        "w_out": nrm(ks[22], (DEPTH, D_MODEL, D_MODEL), D_MODEL ** -0.5),
        "w_router_group": nrm(ks[23], (DEPTH, D_MODEL, N_GROUPS), D_MODEL ** -0.5),
        "b_router_group": nrm(ks[24], (DEPTH, N_GROUPS), 0.01),
        "w_router_expert": nrm(ks[25], (DEPTH, D_MODEL, N_EXPERTS), D_MODEL ** -0.5),
        "b_router_expert": nrm(ks[26], (DEPTH, N_EXPERTS), 0.01),
        "w_exp_gate": nrm(ks[27], (DEPTH, N_EXPERTS, D_MODEL, E_FF), D_MODEL ** -0.5),
        "w_exp_up": nrm(ks[28], (DEPTH, N_EXPERTS, D_MODEL, E_FF), D_MODEL ** -0.5),
        "w_exp_down": nrm(ks[29], (DEPTH, N_EXPERTS, E_FF, D_MODEL), E_FF ** -0.5),
    }


def reference(x_prompt, x_sample, cache_diff_k, cache_diff_v, state_ret, c, c_ctx,
              w_mod, b_mod, g_norm1, g_norm2, w_in, g_qnorm, g_knorm, lambda_qk, g_subln,
              ret_decay_logit, conv_w, conv_b, conv_ln_g, conv_ln_b, w_branch, w_out,
              w_router_group, b_router_group, w_router_expert, b_router_expert,
              w_exp_gate, w_exp_up, w_exp_down):
    yp = x_prompt
    ys = x_sample
    new_k, new_v, new_s = [], [], []
    for l in range(DEPTH):
        lw = (w_in[l], g_norm1[l], g_norm2[l], g_qnorm[l], g_knorm[l], lambda_qk[l], g_subln[l],
              ret_decay_logit[l], conv_w[l], conv_b[l], conv_ln_g[l], conv_ln_b[l],
              w_branch[l], w_out[l], w_router_group[l], b_router_group[l],
              w_router_expert[l], b_router_expert[l], w_exp_gate[l], w_exp_up[l], w_exp_down[l])
        mods_ctx = ada_params(c_ctx[None, :], w_mod[l], b_mod[l])
        mods_lat = ada_params(c, w_mod[l], b_mod[l])
        yp, (k_l, v_l, s_l) = trunk_layer(yp, mods_ctx, lw, l)
        new_k.append(k_l)
        new_v.append(v_l)
        new_s.append(s_l)
        ys, _ = trunk_layer(ys, mods_lat, lw, l,
                            ctx=(cache_diff_k[:, l], cache_diff_v[:, l], state_ret[:, l]))
    new_diff_k = jnp.stack(new_k, axis=1)
    new_diff_v = jnp.stack(new_v, axis=1)
    new_state_ret = jnp.stack(new_s, axis=1)
    return (yp, ys, new_diff_k, new_diff_v, new_state_ret)
```

```python
import functools
import math

import jax
import jax.numpy as jnp
from jax import lax
from jax.experimental import pallas as pl
from jax.experimental.pallas import tpu as pltpu

F32 = jnp.float32
BF16 = jnp.bfloat16

EPS = 1e-6
GRID_W = 64
ROPE_BASE = 10000.0
HEAD_W = 128
SUB_W = 64
N_HEADS = 8
CHUNK = 128
CONV_K = 31
CONV_HALO = 16
N_EXPERTS = 16
EXP_PER_GROUP = 4
N_GROUPS = 4
LANES = 128
NEG = -1e30
VMEM_LIMIT = 56 * 1024 * 1024


def _cparams(sem):
    return pltpu.CompilerParams(dimension_semantics=sem, vmem_limit_bytes=VMEM_LIMIT)


def _lane(shape):
    return lax.broadcasted_iota(jnp.int32, shape, len(shape) - 1)


def _sigmoid(x):
    return 1.0 / (1.0 + jnp.exp(-x))


def _silu(x):
    return x * _sigmoid(x)


def _dot(a, b):
    return jnp.dot(a, b, preferred_element_type=F32)


def _dot_nt(a, b):
    return lax.dot_general(a, b, (((1,), (1,)), ((), ())), preferred_element_type=F32)


def _mods_kernel(c_ref, w_ref, b_ref, o_ref):
    s = _silu(c_ref[...])
    o_ref[0] = jnp.dot(s, w_ref[0], preferred_element_type=F32,
                       precision=lax.Precision.HIGHEST) + b_ref[0]


def _mods(cond8, w_mod, b_mod):
    depth, d, n = w_mod.shape
    tn = 1024
    return pl.pallas_call(
        _mods_kernel,
        out_shape=jax.ShapeDtypeStruct((depth, 8, n), F32),
        grid=(depth, n // tn),
        in_specs=[pl.BlockSpec((8, d), lambda l, j: (0, 0)),
                  pl.BlockSpec((1, d, tn), lambda l, j: (l, 0, j)),
                  pl.BlockSpec((1, 1, tn), lambda l, j: (l, 0, j))],
        out_specs=pl.BlockSpec((1, 8, tn), lambda l, j: (l, 0, j)),
        compiler_params=_cparams(("parallel", "parallel")),
        name="ada_mods",
    )(cond8, w_mod, b_mod.reshape(depth, 1, n))


def _inproj_kernel(x_ref, g_ref, sh_ref, sc_ref, w_ref, o_ref, h_scr):
    @pl.when(pl.program_id(1) == 0)
    def _():
        x = x_ref[...]
        r = lax.rsqrt(jnp.mean(x * x, axis=-1, keepdims=True) + EPS)
        h = x * r * g_ref[...]
        h_scr[...] = (h * (1.0 + sc_ref[0]) + sh_ref[0]).astype(BF16)

    o_ref[...] = _dot(h_scr[...], w_ref[...]).astype(o_ref.dtype)


def _inproj(x, g, mods3, w_bf, set_of_row, tm=1024, tn=1024):
    nt, d = x.shape
    n = w_bf.shape[1]
    return pl.pallas_call(
        _inproj_kernel,
        out_shape=jax.ShapeDtypeStruct((nt, n), BF16),
        grid=(nt // tm, n // tn),
        in_specs=[pl.BlockSpec((tm, d), lambda i, j: (i, 0)),
                  pl.BlockSpec((1, d), lambda i, j: (0, 0)),
                  pl.BlockSpec((1, 1, d), lambda i, j: (set_of_row(i * tm), 0, 0)),
                  pl.BlockSpec((1, 1, d), lambda i, j: (set_of_row(i * tm), 0, 1)),
                  pl.BlockSpec((d, tn), lambda i, j: (0, j))],
        out_specs=pl.BlockSpec((tm, tn), lambda i, j: (i, j)),
        scratch_shapes=[pltpu.VMEM((tm, d), BF16)],
        compiler_params=_cparams(("parallel", "arbitrary")),
        name="inproj",
    )(x, g, mods3, mods3, w_bf)


def _qk_norm(x, g):
    lo = _lane(x.shape) < SUB_W
    x2 = x * x
    s_lo = jnp.sum(jnp.where(lo, x2, 0.0), axis=-1, keepdims=True)
    s_hi = jnp.sum(jnp.where(lo, 0.0, x2), axis=-1, keepdims=True)
    r = jnp.where(lo, lax.rsqrt(s_lo / SUB_W + EPS), lax.rsqrt(s_hi / SUB_W + EPS))
    return x * r * g


def _rope(x, cos, sin_signed):
    first = (_lane(x.shape) % 32) < 16
    partner = jnp.where(first, pltpu.roll(x, LANES - 16, 1), pltpu.roll(x, 16, 1))
    return x * cos + partner * sin_signed


def _attn_kernel(*refs, tq, n_cache, rope, emit_kv, lam_init):
    it = iter(refs)
    lam_ref, gq_ref, gk_ref, gs_ref = next(it), next(it), next(it), next(it)
    q_ref, k_ref, v_ref = next(it), next(it), next(it)
    if n_cache:
        ck_ref, cv_ref = next(it), next(it)
    if rope:
        cos_ref, sin_ref = next(it), next(it)
    o_ref = next(it)
    if emit_kv:
        ak_ref, av_ref = next(it), next(it)
    kn_scr, v_scr = next(it), next(it)

    qi = pl.program_id(2)

    @pl.when(qi == 0)
    def _():
        kn = _qk_norm(k_ref[...].astype(F32), gk_ref[...])
        if emit_kv:
            ak_ref[...] = kn
            av_ref[...] = v_ref[...].astype(F32)
        if rope:
            kn = _rope(kn, cos_ref[...], sin_ref[...])
        if n_cache:
            kn_scr[:n_cache, :] = ck_ref[...].astype(BF16)
            v_scr[:n_cache, :] = cv_ref[...].astype(BF16)
        kn_scr[n_cache:, :] = kn.astype(BF16)
        v_scr[n_cache:, :] = v_ref[...]

    lp = lam_ref[...]
    lam = (jnp.exp(jnp.sum(lp[0:1] * lp[1:2], axis=-1, keepdims=True))
           - jnp.exp(jnp.sum(lp[2:3] * lp[3:4], axis=-1, keepdims=True)) + lam_init)

    qn = _qk_norm(q_ref[...].astype(F32), gq_ref[...])
    if rope:
        r0 = pl.multiple_of(qi * tq, tq)
        qn = _rope(qn, cos_ref[pl.ds(r0, tq), :], sin_ref[pl.ds(r0, tq), :])
    qn = qn * (SUB_W ** -0.5)
    lo = _lane(qn.shape) < SUB_W
    q0 = jnp.where(lo, qn, 0.0).astype(BF16)
    q1 = jnp.where(lo, 0.0, qn).astype(BF16)
    kn = kn_scr[...]

    def softmax(q):
        s = _dot_nt(q, kn)
        p = jnp.exp(s - jnp.max(s, axis=-1, keepdims=True))
        return p, 1.0 / jnp.sum(p, axis=-1, keepdims=True)

    p0, r0_ = softmax(q0)
    p1, r1_ = softmax(q1)
    a = p0 * r0_ - p1 * (lam * r1_)
    o = _dot(a.astype(BF16), v_scr[...])
    o = o * lax.rsqrt(jnp.mean(o * o, axis=-1, keepdims=True) + EPS) * (gs_ref[...] * (1.0 - lam_init))
    o_ref[...] = o.astype(o_ref.dtype)


def _attention(z, lam_p, gq, gk, gs, *, row0, n_seq, t, tq, lam_init, cache=None, rope_tabs=None,
               emit_kv=False):
    n_cache = 0 if cache is None else cache[0].shape[1]
    nq = t // tq
    rb = row0 // t
    rq = row0 // tq
    kcol, vcol = N_HEADS, 2 * N_HEADS
    in_specs = [pl.BlockSpec((4, SUB_W), lambda b, h, i: (0, 0)),
                pl.BlockSpec((1, HEAD_W), lambda b, h, i: (0, 0)),
                pl.BlockSpec((1, HEAD_W), lambda b, h, i: (0, 0)),
                pl.BlockSpec((1, HEAD_W), lambda b, h, i: (0, 0)),
                pl.BlockSpec((tq, HEAD_W), lambda b, h, i: (rq + b * nq + i, h)),
                pl.BlockSpec((t, HEAD_W), lambda b, h, i: (rb + b, kcol + h)),
                pl.BlockSpec((t, HEAD_W), lambda b, h, i: (rb + b, vcol + h))]
    args = [lam_p, gq, gk, gs, z, z, z]
    if n_cache:
        in_specs += [pl.BlockSpec((None, n_cache, HEAD_W), lambda b, h, i: (b, 0, h)),
                     pl.BlockSpec((None, n_cache, HEAD_W), lambda b, h, i: (b, 0, h))]
        args += [cache[0], cache[1]]
    if rope_tabs is not None:
        in_specs += [pl.BlockSpec((t, HEAD_W), lambda b, h, i: (0, 0)),
                     pl.BlockSpec((t, HEAD_W), lambda b, h, i: (0, 0))]
        args += list(rope_tabs)
    out_shape = [jax.ShapeDtypeStruct((n_seq * t, N_HEADS * HEAD_W), BF16)]
    out_specs = [pl.BlockSpec((tq, HEAD_W), lambda b, h, i: (b * nq + i, h))]
    if emit_kv:
        out_shape += [jax.ShapeDtypeStruct((n_seq * t, N_HEADS * HEAD_W), F32)] * 2
        out_specs += [pl.BlockSpec((t, HEAD_W), lambda b, h, i: (b, h))] * 2
    return pl.pallas_call(
        functools.partial(_attn_kernel, tq=tq, n_cache=n_cache, rope=rope_tabs is not None,
                          emit_kv=emit_kv, lam_init=lam_init),
        out_shape=out_shape,
        grid=(n_seq, N_HEADS, nq),
        in_specs=in_specs,
        out_specs=out_specs,
        scratch_shapes=[pltpu.VMEM((n_cache + t, HEAD_W), BF16),
                        pltpu.VMEM((n_cache + t, HEAD_W), BF16)],
        compiler_params=_cparams(("parallel", "parallel", "arbitrary")),
        name="diff_attn_lat" if n_cache else "diff_attn_ctx",
    )(*args)


def _log_sigmoid(x):
    return jnp.minimum(x, 0.0) - jnp.log1p(jnp.exp(-jnp.abs(x)))


def _group_norm(o):
    mu = jnp.mean(o, axis=-1, keepdims=True)
    d = o - mu
    return d * lax.rsqrt(jnp.mean(d * d, axis=-1, keepdims=True) + EPS)


def _ret_kernel(*refs, n_chunks, has_s0, emit_state):
    it = iter(refs)
    dlf_ref, dlb_ref = next(it), next(it)
    q_ref, k_ref, v_ref, gf_ref, gb_ref = (next(it) for _ in range(5))
    if has_s0:
        s0f_ref, s0b_ref = next(it), next(it)
    y_ref = next(it)
    if emit_state:
        st_ref = next(it)
    of_scr = next(it)

    c = CHUNK
    row = lax.broadcasted_iota(jnp.int32, (c, c), 0).astype(F32)
    col = lax.broadcasted_iota(jnp.int32, (c, c), 1).astype(F32)
    k_scale = HEAD_W ** -0.5
    lgf = _log_sigmoid(dlf_ref[0])
    lgb = _log_sigmoid(dlb_ref[0])
    diff = row - col
    d_f = jnp.where(diff >= 0, jnp.exp(jnp.maximum(diff, 0.0) * lgf), 0.0) * k_scale
    d_b = jnp.where(diff <= 0, jnp.exp(jnp.maximum(-diff, 0.0) * lgb), 0.0) * k_scale
    qdec_f = jnp.exp((row + 1.0) * lgf)
    kdec_f = jnp.exp((c - 1.0 - row) * lgf) * k_scale
    qdec_b = jnp.exp((c - row) * lgb)
    kdec_b = jnp.exp(row * lgb) * k_scale
    cdec_f = jnp.exp(c * lgf)
    cdec_b = jnp.exp(c * lgb)

    def chunk(ci, s, d_in, qdec, kdec, cdec):
        r = pl.multiple_of(ci * c, c)
        qc = q_ref[pl.ds(r, c), :]
        kc = k_ref[pl.ds(r, c), :]
        vc = v_ref[pl.ds(r, c), :]
        att = _dot_nt(qc, kc) * d_in
        o = _dot(att.astype(BF16), vc) + _dot(qc, s.astype(BF16)) * qdec
        kd = (kc.astype(F32) * kdec).T.astype(BF16)
        s = s * cdec + _dot(kd, vc)
        return r, o, s

    def fwd(ci, s):
        r, o, s = chunk(ci, s, d_f, qdec_f, kdec_f, cdec_f)
        of_scr[pl.ds(r, c), :] = o
        return s

    def bwd(i, s):
        r, o, s = chunk(n_chunks - 1 - i, s, d_b, qdec_b, kdec_b, cdec_b)
        y = (_group_norm(of_scr[pl.ds(r, c), :]) * _silu(gf_ref[pl.ds(r, c), :].astype(F32))
             + _group_norm(o) * _silu(gb_ref[pl.ds(r, c), :].astype(F32)))
        y_ref[pl.ds(r, c), :] = y.astype(y_ref.dtype)
        return s

    zero = jnp.zeros((c, c), F32)
    sf = lax.fori_loop(0, n_chunks, fwd, s0f_ref[...] if has_s0 else zero)
    sb = lax.fori_loop(0, n_chunks, bwd, s0b_ref[...] if has_s0 else zero)
    if emit_state:
        st_ref[0] = sf
        st_ref[1] = sb


def _retention(z, decay16, *, row0, n_seq, t, s0=None, emit_state=False):
    rb = row0 // t
    col = lambda j: (lambda b, h: (rb + b, j * N_HEADS + h))
    in_specs = [pl.BlockSpec((1, 1, 1), lambda b, h: (h, 0, 0)),
                pl.BlockSpec((1, 1, 1), lambda b, h: (N_HEADS + h, 0, 0))]
    in_specs += [pl.BlockSpec((t, HEAD_W), col(j)) for j in (3, 4, 5, 6, 7)]
    args = [decay16, decay16, z, z, z, z, z]
    if s0 is not None:
        in_specs += [pl.BlockSpec((None, None, None, CHUNK, CHUNK), lambda b, h: (b, 0, h, 0, 0)),
                     pl.BlockSpec((None, None, None, CHUNK, CHUNK), lambda b, h: (b, 1, h, 0, 0))]
        args += [s0, s0]
    out_shape = [jax.ShapeDtypeStruct((n_seq * t, N_HEADS * HEAD_W), BF16)]
    out_specs = [pl.BlockSpec((t, HEAD_W), lambda b, h: (b, h))]
    if emit_state:
        out_shape.append(jax.ShapeDtypeStruct((n_seq, 2, N_HEADS, CHUNK, CHUNK), F32))
        out_specs.append(pl.BlockSpec((None, 2, None, CHUNK, CHUNK), lambda b, h: (b, 0, h, 0, 0)))
    return pl.pallas_call(
        functools.partial(_ret_kernel, n_chunks=t // CHUNK, has_s0=s0 is not None,
                          emit_state=emit_state),
        out_shape=out_shape,
        grid=(n_seq, N_HEADS),
        in_specs=in_specs,
        out_specs=out_specs,
        scratch_shapes=[pltpu.VMEM((t, HEAD_W), F32)],
        compiler_params=_cparams(("parallel", "parallel")),
        name="retention_lat" if s0 is not None else "retention_ctx",
    )(*args)


def _conv_kernel(a_ref, g_ref, ap_ref, gp_ref, an_ref, gn_ref, w_ref, b_ref, lg_ref, lb_ref,
                 o_ref, u_scr, sh_scr, y_scr, *, tt, nt, rblk):
    ti = pl.program_id(1)
    halo = CONV_HALO
    cw = a_ref.shape[1]

    def glu(a, g):
        return a[...].astype(F32) * _sigmoid(g[...].astype(F32))

    u_scr[halo:halo + tt, :] = glu(a_ref, g_ref)
    u_scr[0:halo, :] = jnp.where(ti > 0, glu(ap_ref, gp_ref), 0.0)
    u_scr[halo + tt:, :] = jnp.where(ti < nt - 1, glu(an_ref, gn_ref), 0.0)

    ext = tt + 2 * halo
    for cb in range(cw // LANES):
        cs = slice(cb * LANES, (cb + 1) * LANES)
        x = u_scr[:, cs]
        sh_scr[0] = x
        for b in range(1, 8):
            sh_scr[b] = pltpu.roll(x, ext - b, 0)

        def rows(ri, carry):
            r = pl.multiple_of(ri * rblk, rblk)
            acc = jnp.zeros((rblk, LANES), F32)
            for k in range(CONV_K):
                off = k + 1 + halo - (CONV_K // 2) - 1
                a8, b8 = off // 8, off % 8
                acc = acc + sh_scr[b8, pl.ds(r + 8 * a8, rblk), :] * w_ref[k:k + 1, cs]
            y_scr[pl.ds(r, rblk), cs] = acc + b_ref[:, cs]
            return carry

        lax.fori_loop(0, tt // rblk, rows, 0)

    y = y_scr[...]
    mu = jnp.mean(y, axis=-1, keepdims=True)
    d = y - mu
    yn = d * lax.rsqrt(jnp.mean(d * d, axis=-1, keepdims=True) + EPS) * lg_ref[...] + lb_ref[...]
    o_ref[...] = _silu(yn).astype(o_ref.dtype)


def _conv(z, w, b, lg, lb, *, row0, n_seq, t, tt):
    cw = w.shape[1]
    nt = t // tt
    total_h = z.shape[0] // CONV_HALO
    acol = 8192 // cw
    r_t = row0 // tt
    r_h = row0 // CONV_HALO
    per_h = tt // CONV_HALO

    def cur(j):
        return lambda s, i: (r_t + s * nt + i, acol + j)

    def prev(j):
        return lambda s, i: (jnp.maximum(r_h + (s * nt + i) * per_h - 1, 0), acol + j)

    def nxt(j):
        return lambda s, i: (jnp.minimum(r_h + (s * nt + i + 1) * per_h, total_h - 1), acol + j)

    full = lambda s, i: (0, 0)
    return pl.pallas_call(
        functools.partial(_conv_kernel, tt=tt, nt=nt, rblk=64),
        out_shape=jax.ShapeDtypeStruct((n_seq * t, cw), BF16),
        grid=(n_seq, nt),
        in_specs=[pl.BlockSpec((tt, cw), cur(0)), pl.BlockSpec((tt, cw), cur(1)),
                  pl.BlockSpec((CONV_HALO, cw), prev(0)), pl.BlockSpec((CONV_HALO, cw), prev(1)),
                  pl.BlockSpec((CONV_HALO, cw), nxt(0)), pl.BlockSpec((CONV_HALO, cw), nxt(1)),
                  pl.BlockSpec((CONV_K, cw), full), pl.BlockSpec((1, cw), full),
                  pl.BlockSpec((1, cw), full), pl.BlockSpec((1, cw), full)],
        out_specs=pl.BlockSpec((tt, cw), lambda s, i: (s * nt + i, 0)),
        scratch_shapes=[pltpu.VMEM((tt + 2 * CONV_HALO, cw), F32),
                        pltpu.VMEM((8, tt + 2 * CONV_HALO, LANES), F32),
                        pltpu.VMEM((tt, cw), F32)],
        compiler_params=_cparams(("parallel", "parallel")),
        name="conformer_conv",
    )(z, z, z, z, z, z, w, b, lg, lb)


def _merge_kernel(a_ref, r_ref, c_ref, g0_ref, g1_ref, g2_ref, w_ref, o_ref):
    acc = _sigmoid(g0_ref[...].astype(F32)) * _dot(a_ref[...], w_ref[0])
    acc += _sigmoid(g1_ref[...].astype(F32)) * _dot(r_ref[...], w_ref[1])
    acc += _sigmoid(g2_ref[...].astype(F32)) * _dot(c_ref[...], w_ref[2])
    o_ref[...] = acc.astype(o_ref.dtype)


def _merge(oa, orr, oc, z, w_br, tm=512, tn=1024):
    nt, bw = oa.shape
    d = w_br.shape[2]
    g0 = 10240 // tn
    gs = d // tn
    br = pl.BlockSpec((tm, bw), lambda i, j: (i, 0))
    return pl.pallas_call(
        _merge_kernel,
        out_shape=jax.ShapeDtypeStruct((nt, d), BF16),
        grid=(nt // tm, d // tn),
        in_specs=[br, br, br,
                  pl.BlockSpec((tm, tn), lambda i, j: (i, g0 + j)),
                  pl.BlockSpec((tm, tn), lambda i, j: (i, g0 + gs + j)),
                  pl.BlockSpec((tm, tn), lambda i, j: (i, g0 + 2 * gs + j)),
                  pl.BlockSpec((3, bw, tn), lambda i, j: (0, 0, j))],
        out_specs=pl.BlockSpec((tm, tn), lambda i, j: (i, j)),
        compiler_params=_cparams(("parallel", "parallel")),
        name="branch_merge",
    )(oa, orr, oc, z, z, z, w_br)


def _outproj_kernel(m_ref, x_ref, w_ref, g1_ref, gn_ref, sh_ref, sc_ref, wr_ref, br_ref,
                    xo_ref, h_ref, cb_ref):
    x = x_ref[...] + g1_ref[0] * _dot(m_ref[...], w_ref[...])
    xo_ref[...] = x
    h = x * lax.rsqrt(jnp.mean(x * x, axis=-1, keepdims=True) + EPS) * gn_ref[...]
    h = h * (1.0 + sc_ref[0]) + sh_ref[0]
    h_ref[...] = h.astype(h_ref.dtype)

    lg = jnp.dot(h, wr_ref[...], preferred_element_type=F32,
                 precision=lax.Precision.HIGHEST) + br_ref[...]
    lane = _lane(lg.shape).astype(F32)
    big = float(LANES)

    def first_max(v):
        m = jnp.max(v, axis=-1, keepdims=True)
        return m, jnp.min(jnp.where(v == m, lane, big), axis=-1, keepdims=True)

    is_g = (lane >= N_EXPERTS) & (lane < N_EXPERTS + N_GROUPS)
    gl = jnp.where(is_g, lg, NEG)
    gmax, gidx = first_max(gl)
    g_w = 1.0 / jnp.sum(jnp.exp(gl - gmax), axis=-1, keepdims=True)
    e_lo = (gidx - N_EXPERTS) * EXP_PER_GROUP
    is_e = (lane >= e_lo) & (lane < e_lo + EXP_PER_GROUP)
    el = jnp.where(is_e, lg, NEG)
    e1, i1 = first_max(el)
    e2, i2 = first_max(jnp.where(lane == i1, NEG, el))
    p2 = jnp.exp(e2 - e1)
    w1 = g_w / (1.0 + p2)
    cb_ref[...] = jnp.where(lane == i1, w1, 0.0) + jnp.where(lane == i2, w1 * p2, 0.0)


def _outproj(merged, x, w_o, mods3, g2, w_router, b_router, set_of_row, tm=512):
    nt, d = x.shape
    mod = lambda c: pl.BlockSpec((1, 1, d), lambda i: (set_of_row(i * tm), 0, c))
    full = lambda i: (0, 0)
    return pl.pallas_call(
        _outproj_kernel,
        out_shape=[jax.ShapeDtypeStruct((nt, d), F32),
                   jax.ShapeDtypeStruct((nt, d), BF16),
                   jax.ShapeDtypeStruct((nt, LANES), F32)],
        grid=(nt // tm,),
        in_specs=[pl.BlockSpec((tm, d), lambda i: (i, 0)),
                  pl.BlockSpec((tm, d), lambda i: (i, 0)),
                  pl.BlockSpec((d, d), full),
                  mod(2), pl.BlockSpec((1, d), full), mod(3), mod(4),
                  pl.BlockSpec((d, LANES), full), pl.BlockSpec((1, LANES), full)],
        out_specs=[pl.BlockSpec((tm, d), lambda i: (i, 0)),
                   pl.BlockSpec((tm, d), lambda i: (i, 0)),
                   pl.BlockSpec((tm, LANES), lambda i: (i, 0))],
        compiler_params=_cparams(("parallel",)),
        name="outproj_router",
    )(merged, x, w_o, mods3, g2, mods3, mods3, w_router, b_router)


def _moe_kernel(h_ref, cb_ref, x_ref, g2_ref, wg_ref, wu_ref, wd_ref, o_ref):
    e = pl.program_id(1)

    @pl.when(e == 0)
    def _():
        o_ref[...] = jnp.zeros_like(o_ref)

    h = h_ref[...]
    cb = cb_ref[...]
    w_e = jnp.sum(jnp.where(_lane(cb.shape) == e, cb, 0.0), axis=-1, keepdims=True)
    act = _silu(_dot(h, wg_ref[0])) * _dot(h, wu_ref[0]) * w_e
    o_ref[...] += _dot(act.astype(BF16), wd_ref[0])

    @pl.when(e == pl.num_programs(1) - 1)
    def _():
        o_ref[...] = x_ref[...] + g2_ref[0] * o_ref[...]


def _moe(h2, comb, x, mods3, wg, wu, wd, set_of_row, tm=512):
    nt, d = x.shape
    ne, _, ff = wg.shape
    return pl.pallas_call(
        _moe_kernel,
        out_shape=jax.ShapeDtypeStruct((nt, d), F32),
        grid=(nt // tm, ne),
        in_specs=[pl.BlockSpec((tm, d), lambda i, e: (i, 0)),
                  pl.BlockSpec((tm, LANES), lambda i, e: (i, 0)),
                  pl.BlockSpec((tm, d), lambda i, e: (i, 0)),
                  pl.BlockSpec((1, 1, d), lambda i, e: (set_of_row(i * tm), 0, 5)),
                  pl.BlockSpec((1, d, ff), lambda i, e: (e, 0, 0)),
                  pl.BlockSpec((1, d, ff), lambda i, e: (e, 0, 0)),
                  pl.BlockSpec((1, ff, d), lambda i, e: (e, 0, 0))],
        out_specs=pl.BlockSpec((tm, d), lambda i, e: (i, 0)),
        compiler_params=_cparams(("parallel", "arbitrary")),
        name="moe_dense",
    )(h2, comb, x, mods3, wg, wu, wd)


def _rope_tables(t):
    n_freq = SUB_W // 4
    tok = jnp.arange(t)
    row = (tok // GRID_W).astype(F32)
    col = (tok % GRID_W).astype(F32)
    inv = ROPE_BASE ** (-jnp.arange(n_freq, dtype=F32) / n_freq)
    ar, ac = row[:, None] * inv, col[:, None] * inv
    cos = jnp.concatenate([jnp.cos(ar)] * 2 + [jnp.cos(ac)] * 2, axis=1)
    sin = jnp.concatenate([-jnp.sin(ar), jnp.sin(ar), -jnp.sin(ac), jnp.sin(ac)], axis=1)
    return jnp.tile(cos, (1, 2)), jnp.tile(sin, (1, 2))


def kernel(x_prompt, x_sample, cache_diff_k, cache_diff_v, state_ret, c, c_ctx, w_mod, b_mod,
           g_norm1, g_norm2, w_in, g_qnorm, g_knorm, lambda_qk, g_subln, ret_decay_logit, conv_w,
           conv_b, conv_ln_g, conv_ln_b, w_branch, w_out, w_router_group, b_router_group,
           w_router_expert, b_router_expert, w_exp_gate, w_exp_up, w_exp_down):
    bc, tc, d = x_prompt.shape
    bl, tl, _ = x_sample.shape
    depth = w_mod.shape[0]
    past = cache_diff_k.shape[2]
    n_ctx = bc * tc
    assert bl + 1 <= 8 and n_ctx % tl == 0

    def set_of_row(r):
        return jnp.where(r < n_ctx, 0, 1 + (r - n_ctx) // tl)

    tm_big = math.gcd(1024, math.gcd(n_ctx, tl))
    tm_small = math.gcd(512, tm_big)

    x = jnp.concatenate([x_prompt.reshape(n_ctx, d), x_sample.reshape(bl * tl, d)], axis=0)
    cond8 = jnp.zeros((8, d), F32).at[0].set(c_ctx).at[1:1 + bl].set(c)
    mods = _mods(cond8, w_mod, b_mod)
    cos, sin = _rope_tables(tl)
    ck = cache_diff_k.reshape(bl, depth, past, N_HEADS * HEAD_W)
    cv = cache_diff_v.reshape(bl, depth, past, N_HEADS * HEAD_W)

    new_k, new_v, new_s = [], [], []
    for l in range(depth):
        lam_init = 0.8 - 0.6 * math.exp(-0.3 * l)
        mods3 = mods[l].reshape(8, 1, 6 * d)
        gq = jnp.tile(g_qnorm[l], 2)[None]
        gk = jnp.tile(g_knorm[l], 2)[None]
        gs = g_subln[l][None]
        decay16 = ret_decay_logit[l].reshape(2 * N_HEADS, 1, 1)

        z = _inproj(x, g_norm1[l][None], mods3, w_in[l].astype(BF16), set_of_row, tm=tm_big)

        oa_c, ak, av = _attention(z, lambda_qk[l], gq, gk, gs, row0=0, n_seq=bc, t=tc, tq=tc,
                                  lam_init=lam_init, emit_kv=True)
        (oa_l,) = _attention(z, lambda_qk[l], gq, gk, gs, row0=n_ctx, n_seq=bl, t=tl,
                             tq=math.gcd(256, tl), lam_init=lam_init,
                             cache=(ck[:, l], cv[:, l]), rope_tabs=(cos, sin))
        or_c, st = _retention(z, decay16, row0=0, n_seq=bc, t=tc, emit_state=True)
        (or_l,) = _retention(z, decay16, row0=n_ctx, n_seq=bl, t=tl, s0=state_ret[:, l])
        cargs = (conv_w[l], conv_b[l][None], conv_ln_g[l][None], conv_ln_b[l][None])
        oc_c = _conv(z, *cargs, row0=0, n_seq=bc, t=tc, tt=math.gcd(512, tc))
        oc_l = _conv(z, *cargs, row0=n_ctx, n_seq=bl, t=tl, tt=math.gcd(512, tl))

        oa = jnp.concatenate([oa_c, oa_l], axis=0)
        orr = jnp.concatenate([or_c, or_l], axis=0)
        oc = jnp.concatenate([oc_c, oc_l], axis=0)
        merged = _merge(oa, orr, oc, z, w_branch[l].astype(BF16), tm=tm_small)

        w_router = jnp.zeros((d, LANES), F32).at[:, :N_EXPERTS].set(w_router_expert[l])
        w_router = w_router.at[:, N_EXPERTS:N_EXPERTS + N_GROUPS].set(w_router_group[l])
        b_router = jnp.zeros((1, LANES), F32).at[0, :N_EXPERTS].set(b_router_expert[l])
        b_router = b_router.at[0, N_EXPERTS:N_EXPERTS + N_GROUPS].set(b_router_group[l])
        x, h2, comb = _outproj(merged, x, w_out[l].astype(BF16), mods3, g_norm2[l][None],
                               w_router, b_router, set_of_row, tm=tm_small)
        x = _moe(h2, comb, x, mods3, w_exp_gate[l].astype(BF16), w_exp_up[l].astype(BF16),
                 w_exp_down[l].astype(BF16), set_of_row, tm=tm_small)

        new_k.append(ak.reshape(bc, tc, N_HEADS, 2, SUB_W))
        new_v.append(av.reshape(bc, tc, N_HEADS, HEAD_W))
        new_s.append(st)

    yp = x[:n_ctx].reshape(bc, tc, d)
    ys = x[n_ctx:].reshape(bl, tl, d)
    return (yp, ys, jnp.stack(new_k, axis=1), jnp.stack(new_v, axis=1), jnp.stack(new_s, axis=1))
```

```python
import functools
import math

import jax
import jax.numpy as jnp
from jax import lax
from jax.experimental import pallas as pl
from jax.experimental.pallas import tpu as pltpu

F32 = jnp.float32
BF16 = jnp.bfloat16

EPS = 1e-6
LOG2E = 1.4426950408889634
GRID_W = 64
ROPE_BASE = 10000.0
HEAD_W = 128
SUB_W = 64
N_HEADS = 8
RET_HEADS_PER_STEP = 2
CHUNK = 128
CONV_K = 31
CONV_HALO = 16
N_EXPERTS = 16
EXP_PER_GROUP = 4
N_GROUPS = 4
LANES = 128
NEG = -1e30
VMEM_LIMIT = 58 * 1024 * 1024

MOE_SUB = 256
MOE_TILE = 1024
MOE_CHUNK = 128
MOE_ROWS = MOE_TILE + N_GROUPS * MOE_CHUNK
SORT_WIN = MOE_SUB + 16
UNSORT_WIN = MOE_SUB + LANES


def _cparams(sem):
    return pltpu.CompilerParams(dimension_semantics=sem, vmem_limit_bytes=VMEM_LIMIT)


def _lane(shape):
    return lax.broadcasted_iota(jnp.int32, shape, len(shape) - 1)


def _sigmoid(x):
    return 1.0 / (1.0 + jnp.exp(-x))


def _silu(x):
    return x * _sigmoid(x)


def _dot(a, b):
    return jnp.dot(a, b, preferred_element_type=F32)


def _dot_nt(a, b):
    return lax.dot_general(a, b, (((1,), (1,)), ((), ())), preferred_element_type=F32)


def _split_bf16(x, parts):
    out = []
    for _ in range(parts - 1):
        hi = x.astype(BF16)
        out.append(hi)
        x = x - hi.astype(F32)
    out.append(x.astype(BF16))
    return out


_ANY = pl.BlockSpec(memory_space=pl.ANY)


def _mods_kernel(c_ref, w_ref, b_ref, o_ref):
    s = _silu(c_ref[...])
    o_ref[0] = jnp.dot(s, w_ref[0], preferred_element_type=F32,
                       precision=lax.Precision.HIGHEST) + b_ref[0]


def _mods(cond8, w_mod, b_mod):
    depth, d, n = w_mod.shape
    tn = 1024
    return pl.pallas_call(
        _mods_kernel,
        out_shape=jax.ShapeDtypeStruct((depth, 8, n), F32),
        grid=(depth, n // tn),
        in_specs=[pl.BlockSpec((8, d), lambda l, j: (0, 0)),
                  pl.BlockSpec((1, d, tn), lambda l, j: (l, 0, j)),
                  pl.BlockSpec((1, 1, tn), lambda l, j: (l, 0, j))],
        out_specs=pl.BlockSpec((1, 8, tn), lambda l, j: (l, 0, j)),
        compiler_params=_cparams(("parallel", "parallel")),
        name="ada_mods",
    )(cond8, w_mod, b_mod.reshape(depth, 1, n))


def _inproj_kernel(x_ref, g_ref, sh_ref, sc_ref, w_ref, o_ref, h_scr):
    @pl.when(pl.program_id(1) == 0)
    def _():
        x = x_ref[...]
        r = lax.rsqrt(jnp.mean(x * x, axis=-1, keepdims=True) + EPS)
        h = x * r * g_ref[...]
        h_scr[...] = (h * (1.0 + sc_ref[0]) + sh_ref[0]).astype(BF16)

    o_ref[...] = _dot(h_scr[...], w_ref[...]).astype(o_ref.dtype)


def _inproj(x, g, mods3, w_bf, set_of_row, tm=1024, tn=1024):
    nt, d = x.shape
    n = w_bf.shape[1]
    return pl.pallas_call(
        _inproj_kernel,
        out_shape=jax.ShapeDtypeStruct((nt, n), BF16),
        grid=(nt // tm, n // tn),
        in_specs=[pl.BlockSpec((tm, d), lambda i, j: (i, 0)),
                  pl.BlockSpec((1, d), lambda i, j: (0, 0)),
                  pl.BlockSpec((1, 1, d), lambda i, j: (set_of_row(i * tm), 0, 0)),
                  pl.BlockSpec((1, 1, d), lambda i, j: (set_of_row(i * tm), 0, 1)),
                  pl.BlockSpec((d, tn), lambda i, j: (0, j))],
        out_specs=pl.BlockSpec((tm, tn), lambda i, j: (i, j)),
        scratch_shapes=[pltpu.VMEM((tm, d), BF16)],
        compiler_params=_cparams(("parallel", "arbitrary")),
        name="inproj",
    )(x, g, mods3, mods3, w_bf)


def _qk_norm(x, g):
    lo = _lane(x.shape) < SUB_W
    x2 = x * x
    s_lo = jnp.sum(jnp.where(lo, x2, 0.0), axis=-1, keepdims=True)
    s_hi = jnp.sum(jnp.where(lo, 0.0, x2), axis=-1, keepdims=True)
    r = jnp.where(lo, lax.rsqrt(s_lo / SUB_W + EPS), lax.rsqrt(s_hi / SUB_W + EPS))
    return x * r * g


def _rope(x, cos, sin_signed):
    first = (_lane(x.shape) % 32) < 16
    partner = jnp.where(first, pltpu.roll(x, LANES - 16, 1), pltpu.roll(x, 16, 1))
    return x * cos + partner * sin_signed


def _attn_kernel(*refs, tq, n_cache, rope, emit_kv, n_alias, lam_init):
    it = iter(refs)
    lam_ref, gq_ref, gk_ref, gs_ref = next(it), next(it), next(it), next(it)
    q_ref, k_ref, v_ref = next(it), next(it), next(it)
    if n_cache:
        ck_ref, cv_ref = next(it), next(it)
    if rope:
        cos_ref, sin_ref = next(it), next(it)
    for _ in range(n_alias):
        next(it)
    o_ref = next(it)
    if emit_kv:
        ak_ref, av_ref = next(it), next(it)
    kn_scr, v_scr = next(it), next(it)

    qi = pl.program_id(2)

    @pl.when(qi == 0)
    def _():
        kn = _qk_norm(k_ref[...].astype(F32), gk_ref[...])
        if emit_kv:
            ak_ref[...] = kn
            av_ref[...] = v_ref[...].astype(F32)
        if rope:
            kn = _rope(kn, cos_ref[...], sin_ref[...])
        if n_cache:
            kn_scr[:n_cache, :] = ck_ref[...].astype(BF16)
            v_scr[:n_cache, :HEAD_W] = cv_ref[...].astype(BF16)
        kn_scr[n_cache:, :] = kn.astype(BF16)
        v_scr[n_cache:, :HEAD_W] = v_ref[...]
        v_scr[:, HEAD_W:] = jnp.ones((v_scr.shape[0], HEAD_W), BF16)

    lp = lam_ref[...]
    lam = (jnp.exp(jnp.sum(lp[0:1] * lp[1:2], axis=-1, keepdims=True))
           - jnp.exp(jnp.sum(lp[2:3] * lp[3:4], axis=-1, keepdims=True)) + lam_init)

    qn = _qk_norm(q_ref[...].astype(F32), gq_ref[...])
    if rope:
        r0 = pl.multiple_of(qi * tq, tq)
        qn = _rope(qn, cos_ref[pl.ds(r0, tq), :], sin_ref[pl.ds(r0, tq), :])
    qn = qn * (SUB_W ** -0.5 * LOG2E)
    lo = _lane(qn.shape) < SUB_W
    kn = kn_scr[...]
    va = v_scr[...]

    def half(q):
        s = _dot_nt(q.astype(BF16), kn)
        p = jnp.exp2(s - jnp.max(s, axis=-1, keepdims=True))
        ov = _dot(p.astype(BF16), va)
        return ov[:, :HEAD_W] / ov[:, HEAD_W:]

    o = half(jnp.where(lo, qn, 0.0)) - lam * half(jnp.where(lo, 0.0, qn))
    o = o * lax.rsqrt(jnp.mean(o * o, axis=-1, keepdims=True) + EPS) * (gs_ref[...] * (1.0 - lam_init))
    o_ref[...] = o.astype(o_ref.dtype)


def _attention(z, lam_p, gq, gk, gs, *, layer, row0, n_seq, t, tq, lam_init, cache=None,
               rope_tabs=None, oa_prev=None, kv_prev=None, kv_shape=None):
    n_cache = 0 if cache is None else cache[0].shape[2]
    emit_kv = kv_shape is not None
    nq = t // tq
    rb = row0 // t
    rq = row0 // tq
    kcol, vcol = N_HEADS, 2 * N_HEADS
    in_specs = [pl.BlockSpec((4, SUB_W), lambda b, h, i: (0, 0)),
                pl.BlockSpec((1, HEAD_W), lambda b, h, i: (0, 0)),
                pl.BlockSpec((1, HEAD_W), lambda b, h, i: (0, 0)),
                pl.BlockSpec((1, HEAD_W), lambda b, h, i: (0, 0)),
                pl.BlockSpec((tq, HEAD_W), lambda b, h, i: (rq + b * nq + i, h)),
                pl.BlockSpec((t, HEAD_W), lambda b, h, i: (rb + b, kcol + h)),
                pl.BlockSpec((t, HEAD_W), lambda b, h, i: (rb + b, vcol + h))]
    args = [lam_p, gq, gk, gs, z, z, z]
    if n_cache:
        cspec = pl.BlockSpec((None, None, n_cache, HEAD_W), lambda b, h, i: (b, layer, 0, h))
        in_specs += [cspec, cspec]
        args += [cache[0], cache[1]]
    if rope_tabs is not None:
        in_specs += [pl.BlockSpec((t, HEAD_W), lambda b, h, i: (0, 0)),
                     pl.BlockSpec((t, HEAD_W), lambda b, h, i: (0, 0))]
        args += list(rope_tabs)
    aliases = {}
    n_alias = 0
    if oa_prev is not None:
        aliases[len(args)] = 0
        in_specs.append(_ANY)
        args.append(oa_prev)
        n_alias += 1
    if kv_prev is not None:
        for j, a in enumerate(kv_prev):
            aliases[len(args)] = 1 + j
            in_specs.append(_ANY)
            args.append(a)
            n_alias += 1
    out_shape = [jax.ShapeDtypeStruct((z.shape[0], N_HEADS * HEAD_W), BF16)]
    out_specs = [pl.BlockSpec((tq, HEAD_W), lambda b, h, i: (rq + b * nq + i, h))]
    if emit_kv:
        out_shape += [jax.ShapeDtypeStruct(kv_shape, F32)] * 2
        out_specs += [pl.BlockSpec((None, None, t, HEAD_W), lambda b, h, i: (b, layer, 0, h))] * 2
    return pl.pallas_call(
        functools.partial(_attn_kernel, tq=tq, n_cache=n_cache, rope=rope_tabs is not None,
                          emit_kv=emit_kv, n_alias=n_alias, lam_init=lam_init),
        out_shape=out_shape,
        grid=(n_seq, N_HEADS, nq),
        in_specs=in_specs,
        out_specs=out_specs,
        scratch_shapes=[pltpu.VMEM((n_cache + t, HEAD_W), BF16),
                        pltpu.VMEM((n_cache + t, 2 * HEAD_W), BF16)],
        input_output_aliases=aliases,
        compiler_params=_cparams(("parallel", "parallel", "arbitrary")),
        name="diff_attn_lat" if n_cache else "diff_attn_ctx",
    )(*args)


def _log_sigmoid(x):
    return jnp.minimum(x, 0.0) - jnp.log1p(jnp.exp(-jnp.abs(x)))


def _group_norm(o):
    mu = jnp.mean(o, axis=-1, keepdims=True)
    d = o - mu
    return d * lax.rsqrt(jnp.mean(d * d, axis=-1, keepdims=True) + EPS)


def _ret_kernel(*refs, n_chunks, has_s0, emit_state, n_alias):
    it = iter(refs)
    dlf_ref, dlb_ref = next(it), next(it)
    q_ref, k_ref, v_ref, gf_ref, gb_ref = (next(it) for _ in range(5))
    if has_s0:
        s0f_ref, s0b_ref = next(it), next(it)
    for _ in range(n_alias):
        next(it)
    y_ref = next(it)
    if emit_state:
        st_ref = next(it)
    of_scr, ob_scr, s_scr, tab_scr = next(it), next(it), next(it), next(it)

    c = CHUNK
    nh = RET_HEADS_PER_STEP
    row = lax.broadcasted_iota(jnp.int32, (c, c), 0).astype(F32)
    col = lax.broadcasted_iota(jnp.int32, (c, c), 1).astype(F32)
    diff = row - col
    k_scale = HEAD_W ** -0.5
    cdec = []
    for hh in range(nh):
        lgf = _log_sigmoid(dlf_ref[hh])
        lgb = _log_sigmoid(dlb_ref[hh])
        tab_scr[hh, 0, 0] = jnp.where(diff >= 0, jnp.exp(jnp.maximum(diff, 0.0) * lgf), 0.0) * k_scale
        tab_scr[hh, 0, 1] = jnp.exp((row + 1.0) * lgf)
        tab_scr[hh, 0, 2] = jnp.exp((c - 1.0 - row) * lgf) * k_scale
        tab_scr[hh, 1, 0] = jnp.where(diff <= 0, jnp.exp(jnp.maximum(-diff, 0.0) * lgb), 0.0) * k_scale
        tab_scr[hh, 1, 1] = jnp.exp((c - row) * lgb)
        tab_scr[hh, 1, 2] = jnp.exp(row * lgb) * k_scale
        cdec.append((jnp.exp(c * lgf), jnp.exp(c * lgb)))
        for dr in range(2):
            if has_s0:
                s_scr[hh, dr] = (s0f_ref, s0b_ref)[dr][hh]
            else:
                s_scr[hh, dr] = jnp.zeros((c, c), F32)

    def step(i, carry):
        for hh in range(nh):
            hs = slice(hh * HEAD_W, (hh + 1) * HEAD_W)
            for dr in range(2):
                ci = i if dr == 0 else n_chunks - 1 - i
                r = pl.multiple_of(ci * c, c)
                qc = q_ref[pl.ds(r, c), hs]
                kc = k_ref[pl.ds(r, c), hs]
                vc = v_ref[pl.ds(r, c), hs]
                s = s_scr[hh, dr]
                att = _dot_nt(qc, kc) * tab_scr[hh, dr, 0]
                o = _dot(att.astype(BF16), vc) + _dot(qc, s.astype(BF16)) * tab_scr[hh, dr, 1]
                kd = (kc.astype(F32) * tab_scr[hh, dr, 2]).T.astype(BF16)
                s_scr[hh, dr] = s * cdec[hh][dr] + _dot(kd, vc)
                (of_scr, ob_scr)[dr][pl.ds(r, c), hs] = o
        return carry

    lax.fori_loop(0, n_chunks, step, 0)

    def combine(ci, carry):
        r = pl.multiple_of(ci * c, c)
        for hh in range(nh):
            hs = slice(hh * HEAD_W, (hh + 1) * HEAD_W)
            y = (_group_norm(of_scr[pl.ds(r, c), hs]) * _silu(gf_ref[pl.ds(r, c), hs].astype(F32))
                 + _group_norm(ob_scr[pl.ds(r, c), hs]) * _silu(gb_ref[pl.ds(r, c), hs].astype(F32)))
            y_ref[pl.ds(r, c), hs] = y.astype(y_ref.dtype)
        return carry

    lax.fori_loop(0, n_chunks, combine, 0)
    if emit_state:
        for hh in range(nh):
            for dr in range(2):
                st_ref[dr, hh] = s_scr[hh, dr]


def _retention(z, decay16, *, layer, row0, n_seq, t, s0=None, y_prev=None, st_prev=None,
               st_shape=None):
    nh = RET_HEADS_PER_STEP
    hw = nh * HEAD_W
    rb = row0 // t
    hp = N_HEADS // nh
    emit_state = st_shape is not None
    col = lambda j: (lambda b, h: (rb + b, j * hp + h))
    in_specs = [pl.BlockSpec((nh, 1, 1), lambda b, h: (h, 0, 0)),
                pl.BlockSpec((nh, 1, 1), lambda b, h: (hp + h, 0, 0))]
    in_specs += [pl.BlockSpec((t, hw), col(j)) for j in (3, 4, 5, 6, 7)]
    args = [decay16, decay16, z, z, z, z, z]
    if s0 is not None:
        for dr in range(2):
            in_specs.append(pl.BlockSpec((None, None, None, nh, CHUNK, CHUNK),
                                         lambda b, h, dr=dr: (b, layer, dr, h, 0, 0)))
        args += [s0, s0]
    aliases = {}
    n_alias = 0
    if y_prev is not None:
        aliases[len(args)] = 0
        in_specs.append(_ANY)
        args.append(y_prev)
        n_alias += 1
    if st_prev is not None:
        aliases[len(args)] = 1
        in_specs.append(_ANY)
        args.append(st_prev)
        n_alias += 1
    out_shape = [jax.ShapeDtypeStruct((z.shape[0], N_HEADS * HEAD_W), BF16)]
    out_specs = [pl.BlockSpec((t, hw), lambda b, h: (rb + b, h))]
    if emit_state:
        out_shape.append(jax.ShapeDtypeStruct(st_shape, F32))
        out_specs.append(pl.BlockSpec((None, None, 2, nh, CHUNK, CHUNK),
                                      lambda b, h: (b, layer, 0, h, 0, 0)))
    return pl.pallas_call(
        functools.partial(_ret_kernel, n_chunks=t // CHUNK, has_s0=s0 is not None,
                          emit_state=emit_state, n_alias=n_alias),
        out_shape=out_shape,
        grid=(n_seq, hp),
        in_specs=in_specs,
        out_specs=out_specs,
        scratch_shapes=[pltpu.VMEM((t, hw), F32), pltpu.VMEM((t, hw), F32),
                        pltpu.VMEM((nh, 2, CHUNK, CHUNK), F32),
                        pltpu.VMEM((nh, 2, 3, CHUNK, CHUNK), F32)],
        input_output_aliases=aliases,
        compiler_params=_cparams(("parallel", "parallel")),
        name="retention_lat" if s0 is not None else "retention_ctx",
    )(*args)


def _conv_kernel(*refs, tt, nt, rblk, n_alias):
    (a_ref, g_ref, ap_ref, gp_ref, an_ref, gn_ref, w_ref, b_ref, lg_ref, lb_ref) = refs[:10]
    o_ref, u_scr, sh_scr, y_scr = refs[10 + n_alias:]
    ti = pl.program_id(1)
    halo = CONV_HALO
    cw = a_ref.shape[1]

    def glu(a, g):
        return a[...].astype(F32) * _sigmoid(g[...].astype(F32))

    u_scr[halo:halo + tt, :] = glu(a_ref, g_ref)
    u_scr[0:halo, :] = jnp.where(ti > 0, glu(ap_ref, gp_ref), 0.0)
    u_scr[halo + tt:, :] = jnp.where(ti < nt - 1, glu(an_ref, gn_ref), 0.0)

    ext = tt + 2 * halo
    for cb in range(cw // LANES):
        cs = slice(cb * LANES, (cb + 1) * LANES)
        x = u_scr[:, cs]
        sh_scr[0] = x
        for b in range(1, 8):
            sh_scr[b] = pltpu.roll(x, ext - b, 0)

        def rows(ri, carry):
            r = pl.multiple_of(ri * rblk, rblk)
            acc = jnp.zeros((rblk, LANES), F32)
            for k in range(CONV_K):
                off = halo - CONV_K // 2 + k
                acc = acc + sh_scr[off % 8, pl.ds(r + 8 * (off // 8), rblk), :] * w_ref[k:k + 1, cs]
            y_scr[pl.ds(r, rblk), cs] = acc + b_ref[:, cs]
            return carry

        lax.fori_loop(0, tt // rblk, rows, 0)

    y = y_scr[...]
    mu = jnp.mean(y, axis=-1, keepdims=True)
    d = y - mu
    yn = d * lax.rsqrt(jnp.mean(d * d, axis=-1, keepdims=True) + EPS) * lg_ref[...] + lb_ref[...]
    o_ref[...] = _silu(yn).astype(o_ref.dtype)


def _conv(z, w, b, lg, lb, *, row0, n_seq, t, tt, y_prev=None):
    cw = w.shape[1]
    nt = t // tt
    total_h = z.shape[0] // CONV_HALO
    acol = 8192 // cw
    r_t = row0 // tt
    r_h = row0 // CONV_HALO
    per_h = tt // CONV_HALO

    def cur(j):
        return lambda s, i: (r_t + s * nt + i, acol + j)

    def prev(j):
        return lambda s, i: (jnp.maximum(r_h + (s * nt + i) * per_h - 1, 0), acol + j)

    def nxt(j):
        return lambda s, i: (jnp.minimum(r_h + (s * nt + i + 1) * per_h, total_h - 1), acol + j)

    full = lambda s, i: (0, 0)
    in_specs = [pl.BlockSpec((tt, cw), cur(0)), pl.BlockSpec((tt, cw), cur(1)),
                pl.BlockSpec((CONV_HALO, cw), prev(0)), pl.BlockSpec((CONV_HALO, cw), prev(1)),
                pl.BlockSpec((CONV_HALO, cw), nxt(0)), pl.BlockSpec((CONV_HALO, cw), nxt(1)),
                pl.BlockSpec((CONV_K, cw), full), pl.BlockSpec((1, cw), full),
                pl.BlockSpec((1, cw), full), pl.BlockSpec((1, cw), full)]
    args = [z, z, z, z, z, z, w, b, lg, lb]
    aliases = {}
    if y_prev is not None:
        aliases[len(args)] = 0
        in_specs.append(_ANY)
        args.append(y_prev)
    return pl.pallas_call(
        functools.partial(_conv_kernel, tt=tt, nt=nt, rblk=64, n_alias=len(aliases)),
        out_shape=jax.ShapeDtypeStruct((z.shape[0], cw), BF16),
        grid=(n_seq, nt),
        in_specs=in_specs,
        out_specs=pl.BlockSpec((tt, cw), lambda s, i: (r_t + s * nt + i, 0)),
        scratch_shapes=[pltpu.VMEM((tt + 2 * CONV_HALO, cw), F32),
                        pltpu.VMEM((8, tt + 2 * CONV_HALO, LANES), F32),
                        pltpu.VMEM((tt, cw), F32)],
        input_output_aliases=aliases,
        compiler_params=_cparams(("parallel", "parallel")),
        name="conformer_conv",
    )(*args)


def _merge_kernel(a_ref, r_ref, c_ref, g0_ref, g1_ref, g2_ref, w_ref, o_ref):
    acc = _sigmoid(g0_ref[...].astype(F32)) * _dot(a_ref[...], w_ref[0])
    acc += _sigmoid(g1_ref[...].astype(F32)) * _dot(r_ref[...], w_ref[1])
    acc += _sigmoid(g2_ref[...].astype(F32)) * _dot(c_ref[...], w_ref[2])
    o_ref[...] = acc.astype(o_ref.dtype)


def _merge(oa, orr, oc, z, w_br, tm=512, tn=1024):
    nt, bw = oa.shape
    d = w_br.shape[2]
    g0 = 10240 // tn
    gs = d // tn
    br = pl.BlockSpec((tm, bw), lambda i, j: (i, 0))
    return pl.pallas_call(
        _merge_kernel,
        out_shape=jax.ShapeDtypeStruct((nt, d), BF16),
        grid=(nt // tm, d // tn),
        in_specs=[br, br, br,
                  pl.BlockSpec((tm, tn), lambda i, j: (i, g0 + j)),
                  pl.BlockSpec((tm, tn), lambda i, j: (i, g0 + gs + j)),
                  pl.BlockSpec((tm, tn), lambda i, j: (i, g0 + 2 * gs + j)),
                  pl.BlockSpec((3, bw, tn), lambda i, j: (0, 0, j))],
        out_specs=pl.BlockSpec((tm, tn), lambda i, j: (i, j)),
        compiler_params=_cparams(("parallel", "parallel")),
        name="branch_merge",
    )(oa, orr, oc, z, z, z, w_br)


def _outproj_kernel(m_ref, x_ref, w_ref, g1_ref, gn_ref, sh_ref, sc_ref, wr_ref, br_ref,
                    xo_ref, h_ref, cb_ref, cnt_ref):
    x = x_ref[...] + g1_ref[0] * _dot(m_ref[...], w_ref[...])
    xo_ref[...] = x
    h = x * lax.rsqrt(jnp.mean(x * x, axis=-1, keepdims=True) + EPS) * gn_ref[...]
    h = h * (1.0 + sc_ref[0]) + sh_ref[0]
    h_hi = h.astype(BF16)
    h_ref[...] = h_hi
    h_lo = (h - h_hi.astype(F32)).astype(BF16)

    lg = _dot(h_hi, wr_ref[0]) + _dot(h_hi, wr_ref[1]) + _dot(h_lo, wr_ref[0]) + br_ref[...]
    lane = _lane(lg.shape).astype(F32)
    big = float(LANES)

    def first_max(v):
        m = jnp.max(v, axis=-1, keepdims=True)
        return m, jnp.min(jnp.where(v == m, lane, big), axis=-1, keepdims=True)

    is_g = (lane >= N_EXPERTS) & (lane < N_EXPERTS + N_GROUPS)
    gl = jnp.where(is_g, lg, NEG)
    gmax, gidx = first_max(gl)
    g_w = 1.0 / jnp.sum(jnp.exp(gl - gmax), axis=-1, keepdims=True)
    e_lo = (gidx - N_EXPERTS) * EXP_PER_GROUP
    is_e = (lane >= e_lo) & (lane < e_lo + EXP_PER_GROUP)
    el = jnp.where(is_e, lg, NEG)
    e1, i1 = first_max(el)
    e2, i2 = first_max(jnp.where(lane == i1, NEG, el))
    p2 = jnp.exp(e2 - e1)
    w1 = g_w / (1.0 + p2)
    onehot = jnp.where(lane == gidx, 1.0, 0.0)
    cb_ref[...] = jnp.where(lane == i1, w1, 0.0) + jnp.where(lane == i2, w1 * p2, 0.0) + onehot
    for s in range(cnt_ref.shape[0]):
        cnt_ref[s] = jnp.sum(onehot[s * MOE_SUB:(s + 1) * MOE_SUB], axis=0, keepdims=True)


def _outproj(merged, x, w_o, mods3, g2, w_router, b_router, set_of_row, tm=512):
    nt, d = x.shape
    mod = lambda c: pl.BlockSpec((1, 1, d), lambda i: (set_of_row(i * tm), 0, c))
    full = lambda i: (0, 0)
    ns = tm // MOE_SUB
    return pl.pallas_call(
        _outproj_kernel,
        out_shape=[jax.ShapeDtypeStruct((nt, d), F32),
                   jax.ShapeDtypeStruct((nt, d), BF16),
                   jax.ShapeDtypeStruct((nt, LANES), F32),
                   jax.ShapeDtypeStruct((nt // MOE_SUB, 1, LANES), F32)],
        grid=(nt // tm,),
        in_specs=[pl.BlockSpec((tm, d), lambda i: (i, 0)),
                  pl.BlockSpec((tm, d), lambda i: (i, 0)),
                  pl.BlockSpec((d, d), full),
                  mod(2), pl.BlockSpec((1, d), full), mod(3), mod(4),
                  pl.BlockSpec((2, d, LANES), lambda i: (0, 0, 0)), pl.BlockSpec((1, LANES), full)],
        out_specs=[pl.BlockSpec((tm, d), lambda i: (i, 0)),
                   pl.BlockSpec((tm, d), lambda i: (i, 0)),
                   pl.BlockSpec((tm, LANES), lambda i: (i, 0)),
                   pl.BlockSpec((ns, 1, LANES), lambda i: (i, 0, 0))],
        compiler_params=_cparams(("parallel",)),
        name="outproj_router",
    )(merged, x, w_o, mods3, g2, mods3, mods3, w_router, b_router)


def _group_onehot(cb):
    lane = _lane(cb.shape)
    return jnp.where((lane >= N_EXPERTS) & (lane < N_EXPERTS + N_GROUPS), cb, 0.0)


def _window_start(base, win):
    w0 = jnp.minimum((base // 16) * 16, MOE_ROWS - win)
    return pl.multiple_of(w0, 16)


def _moe_sort_kernel(base_ref, h_ref, cb_ref, xs_ref, cs_ref):
    i, s = pl.program_id(0), pl.program_id(1)
    n_sub = pl.num_programs(1)

    @pl.when(s == 0)
    def _():
        xs_ref[...] = jnp.zeros_like(xs_ref)
        cs_ref[...] = jnp.zeros_like(cs_ref)

    cb = cb_ref[...]
    g1h = _group_onehot(cb).astype(BF16)
    eye = (lax.broadcasted_iota(jnp.int32, (LANES, LANES), 0)
           == lax.broadcasted_iota(jnp.int32, (LANES, LANES), 1)).astype(BF16)
    gt = _dot_nt(eye, g1h)
    before = (lax.broadcasted_iota(jnp.int32, (MOE_SUB, MOE_SUB), 0)
              < lax.broadcasted_iota(jnp.int32, (MOE_SUB, MOE_SUB), 1)).astype(BF16)
    rank_t = _dot(gt.astype(BF16), before)
    sub = lax.broadcasted_iota(jnp.int32, (LANES, 1), 0)
    bases = [base_ref[(i * n_sub + s) * N_GROUPS + g] for g in range(N_GROUPS)]
    basec = jnp.zeros((LANES, 1), F32)
    for g in range(N_GROUPS):
        basec = jnp.where(sub == N_EXPERTS + g, bases[g].astype(F32), basec)
    dest = jnp.sum(gt * (basec + rank_t), axis=0, keepdims=True)
    x = h_ref[...]
    cparts = _split_bf16(cb, 3)
    for g in range(N_GROUPS):
        w0 = _window_start(bases[g], SORT_WIN)
        slot = lax.broadcasted_iota(jnp.int32, (SORT_WIN, MOE_SUB), 0).astype(F32) + w0.astype(F32)
        ing = gt[N_EXPERTS + g:N_EXPERTS + g + 1, :]
        perm = jnp.where((slot == dest) & (ing > 0.5), 1.0, 0.0).astype(BF16)
        win = pl.ds(w0, SORT_WIN)
        xs_ref[win, :] = (xs_ref[win, :].astype(F32) + _dot(perm, x)).astype(BF16)
        cs_ref[win, :] += _dot(perm, cparts[0]) + _dot(perm, cparts[1]) + _dot(perm, cparts[2])


def _moe_expert_kernel(c0_ref, nc_ref, xs_ref, cs_ref, wg_ref, wu_ref, wd_ref, ys_ref):
    i, e = pl.program_id(0), pl.program_id(1)
    g = e // EXP_PER_GROUP

    @pl.when(e == 0)
    def _():
        ys_ref[...] = jnp.zeros_like(ys_ref)

    c0 = c0_ref[i * N_GROUPS + g]
    nc = nc_ref[i * N_GROUPS + g]

    def chunk(ci, carry):
        r = pl.multiple_of(ci * MOE_CHUNK, MOE_CHUNK)
        x = xs_ref[pl.ds(r, MOE_CHUNK), :]
        cw = cs_ref[pl.ds(r, MOE_CHUNK), :]
        w_e = jnp.sum(jnp.where(_lane(cw.shape) == e, cw, 0.0), axis=-1, keepdims=True)
        act = _silu(_dot(x, wg_ref[0])) * _dot(x, wu_ref[0]) * w_e
        ys_ref[pl.ds(r, MOE_CHUNK), :] += _dot(act.astype(BF16), wd_ref[0])
        return carry

    lax.fori_loop(c0, c0 + nc, chunk, 0)


def _moe_unsort_kernel(base_ref, ys_ref, cb_ref, x_ref, g2_ref, o_ref, *, tile0):
    i, s = pl.program_id(0), pl.program_id(2)
    n_sub = pl.num_programs(2)
    cb = cb_ref[...]
    g1h = _group_onehot(cb)
    after = (lax.broadcasted_iota(jnp.int32, (MOE_SUB, MOE_SUB), 1)
             < lax.broadcasted_iota(jnp.int32, (MOE_SUB, MOE_SUB), 0)).astype(BF16)
    rank = _dot(after, g1h.astype(BF16))
    lane = _lane((1, LANES))
    bases = [base_ref[((tile0 + i) * n_sub + s) * N_GROUPS + g] for g in range(N_GROUPS)]
    basev = jnp.zeros((1, LANES), F32)
    for g in range(N_GROUPS):
        basev = jnp.where(lane == N_EXPERTS + g, bases[g].astype(F32), basev)
    dest = jnp.sum(g1h * (basev + rank), axis=-1, keepdims=True)
    acc = jnp.zeros(o_ref.shape, F32)
    for g in range(N_GROUPS):
        w0 = _window_start(bases[g], UNSORT_WIN)
        slot = lax.broadcasted_iota(jnp.int32, (MOE_SUB, UNSORT_WIN), 1).astype(F32) + w0.astype(F32)
        ing = jnp.sum(jnp.where(_lane(cb.shape) == N_EXPERTS + g, cb, 0.0), axis=-1, keepdims=True)
        perm_t = jnp.where((slot == dest) & (ing > 0.5), 1.0, 0.0).astype(BF16)
        acc += _dot(perm_t, ys_ref[pl.ds(w0, UNSORT_WIN), :].astype(BF16))
    o_ref[...] = x_ref[...] + g2_ref[0] * acc


def _moe_sort(h2, comb, base):
    nt, d = h2.shape
    n_tiles, n_sub = nt // MOE_TILE, MOE_TILE // MOE_SUB
    return pl.pallas_call(
        _moe_sort_kernel,
        out_shape=[jax.ShapeDtypeStruct((n_tiles * MOE_ROWS, d), BF16),
                   jax.ShapeDtypeStruct((n_tiles * MOE_ROWS, LANES), F32)],
        grid_spec=pltpu.PrefetchScalarGridSpec(
            num_scalar_prefetch=1,
            grid=(n_tiles, n_sub),
            in_specs=[pl.BlockSpec((MOE_SUB, d), lambda i, s, b: (i * n_sub + s, 0)),
                      pl.BlockSpec((MOE_SUB, LANES), lambda i, s, b: (i * n_sub + s, 0))],
            out_specs=[pl.BlockSpec((MOE_ROWS, d), lambda i, s, b: (i, 0)),
                       pl.BlockSpec((MOE_ROWS, LANES), lambda i, s, b: (i, 0))]),
        compiler_params=_cparams(("parallel", "arbitrary")),
        name="moe_sort",
    )(base, h2, comb)


def _moe_experts(xs, cs, c0, nc, wg, wu, wd):
    d = xs.shape[1]
    n_tiles = xs.shape[0] // MOE_ROWS
    ne, _, ff = wg.shape
    return pl.pallas_call(
        _moe_expert_kernel,
        out_shape=jax.ShapeDtypeStruct((n_tiles * MOE_ROWS, d), F32),
        grid_spec=pltpu.PrefetchScalarGridSpec(
            num_scalar_prefetch=2,
            grid=(n_tiles, ne),
            in_specs=[pl.BlockSpec((MOE_ROWS, d), lambda i, e, a, b: (i, 0)),
                      pl.BlockSpec((MOE_ROWS, LANES), lambda i, e, a, b: (i, 0)),
                      pl.BlockSpec((1, d, ff), lambda i, e, a, b: (e, 0, 0)),
                      pl.BlockSpec((1, d, ff), lambda i, e, a, b: (e, 0, 0)),
                      pl.BlockSpec((1, ff, d), lambda i, e, a, b: (e, 0, 0))],
            out_specs=pl.BlockSpec((MOE_ROWS, d), lambda i, e, a, b: (i, 0))),
        compiler_params=_cparams(("parallel", "arbitrary")),
        name="moe_experts",
    )(c0, nc, xs, cs, wg, wu, wd)


def _moe_unsort(ys, comb, x, mods3, base, set_of_row, *, row0, n_rows, tn=1024):
    d = x.shape[1]
    n_sub = MOE_TILE // MOE_SUB
    tile0, sub0 = row0 // MOE_TILE, row0 // MOE_SUB
    return pl.pallas_call(
        functools.partial(_moe_unsort_kernel, tile0=tile0),
        out_shape=jax.ShapeDtypeStruct((n_rows, d), F32),
        grid_spec=pltpu.PrefetchScalarGridSpec(
            num_scalar_prefetch=1,
            grid=(n_rows // MOE_TILE, d // tn, n_sub),
            in_specs=[pl.BlockSpec((MOE_ROWS, tn), lambda i, j, s, b: (tile0 + i, j)),
                      pl.BlockSpec((MOE_SUB, LANES), lambda i, j, s, b: (sub0 + i * n_sub + s, 0)),
                      pl.BlockSpec((MOE_SUB, tn), lambda i, j, s, b: (sub0 + i * n_sub + s, j)),
                      pl.BlockSpec((1, 1, tn),
                                   lambda i, j, s, b: (set_of_row(row0 + i * MOE_TILE), 0,
                                                       5 * (d // tn) + j))],
            out_specs=pl.BlockSpec((MOE_SUB, tn), lambda i, j, s, b: (i * n_sub + s, j))),
        compiler_params=_cparams(("parallel", "parallel", "arbitrary")),
        name="moe_unsort",
    )(base, ys, comb, x, mods3)


def _moe_plan(cnt):
    n_sub = MOE_TILE // MOE_SUB
    c = cnt[:, 0, N_EXPERTS:N_EXPERTS + N_GROUPS].astype(jnp.int32).reshape(-1, n_sub, N_GROUPS)
    seg = (jnp.sum(c, axis=1) + MOE_CHUNK - 1) // MOE_CHUNK * MOE_CHUNK
    seg_start = jnp.cumsum(seg, axis=1) - seg
    base = seg_start[:, None, :] + jnp.cumsum(c, axis=1) - c
    return base.reshape(-1), (seg_start // MOE_CHUNK).reshape(-1), (seg // MOE_CHUNK).reshape(-1)


def _rope_tables(t):
    n_freq = SUB_W // 4
    tok = jnp.arange(t)
    row = (tok // GRID_W).astype(F32)
    col = (tok % GRID_W).astype(F32)
    inv = ROPE_BASE ** (-jnp.arange(n_freq, dtype=F32) / n_freq)
    ar, ac = row[:, None] * inv, col[:, None] * inv
    cos = jnp.concatenate([jnp.cos(ar)] * 2 + [jnp.cos(ac)] * 2, axis=1)
    sin = jnp.concatenate([-jnp.sin(ar), jnp.sin(ar), -jnp.sin(ac), jnp.sin(ac)], axis=1)
    return jnp.tile(cos, (1, 2)), jnp.tile(sin, (1, 2))


def kernel(x_prompt, x_sample, cache_diff_k, cache_diff_v, state_ret, c, c_ctx, w_mod, b_mod,
           g_norm1, g_norm2, w_in, g_qnorm, g_knorm, lambda_qk, g_subln, ret_decay_logit, conv_w,
           conv_b, conv_ln_g, conv_ln_b, w_branch, w_out, w_router_group, b_router_group,
           w_router_expert, b_router_expert, w_exp_gate, w_exp_up, w_exp_down):
    bc, tc, d = x_prompt.shape
    bl, tl, _ = x_sample.shape
    depth = w_mod.shape[0]
    past = cache_diff_k.shape[2]
    n_ctx, n_lat = bc * tc, bl * tl
    assert bl + 1 <= 8 and n_ctx % tl == 0 and n_ctx % MOE_TILE == 0 and tl % MOE_TILE == 0

    def set_of_row(r):
        return jnp.where(r < n_ctx, 0, 1 + (r - n_ctx) // tl)

    tm_big = math.gcd(1024, math.gcd(n_ctx, tl))
    tm_small = math.gcd(512, tm_big)

    x = jnp.concatenate([x_prompt.reshape(n_ctx, d), x_sample.reshape(n_lat, d)], axis=0)
    cond8 = jnp.zeros((8, d), F32).at[0].set(c_ctx).at[1:1 + bl].set(c)
    mods = _mods(cond8, w_mod, b_mod)
    cos, sin = _rope_tables(tl)
    ck = cache_diff_k.reshape(bl, depth, past, N_HEADS * HEAD_W)
    cv = cache_diff_v.reshape(bl, depth, past, N_HEADS * HEAD_W)
    kv_shape = (bc, depth, tc, N_HEADS * HEAD_W)
    st_shape = (bc, depth, 2, N_HEADS, CHUNK, CHUNK)

    new_kv, new_st = None, None
    for l in range(depth):
        lam_init = 0.8 - 0.6 * math.exp(-0.3 * l)
        mods3 = mods[l].reshape(8, 1, 6 * d)
        gq = jnp.tile(g_qnorm[l], 2)[None]
        gk = jnp.tile(g_knorm[l], 2)[None]
        gs = g_subln[l][None]
        decay16 = ret_decay_logit[l].reshape(2 * N_HEADS, 1, 1)

        z = _inproj(x, g_norm1[l][None], mods3, w_in[l].astype(BF16), set_of_row, tm=tm_big)

        oa, ak, av = _attention(z, lambda_qk[l], gq, gk, gs, layer=l, row0=0, n_seq=bc, t=tc,
                                tq=tc, lam_init=lam_init, kv_prev=new_kv, kv_shape=kv_shape)
        new_kv = (ak, av)
        (oa,) = _attention(z, lambda_qk[l], gq, gk, gs, layer=l, row0=n_ctx, n_seq=bl, t=tl,
                           tq=math.gcd(256, tl), lam_init=lam_init, cache=(ck, cv),
                           rope_tabs=(cos, sin), oa_prev=oa)
        orr, new_st = _retention(z, decay16, layer=l, row0=0, n_seq=bc, t=tc, st_prev=new_st,
                                 st_shape=st_shape)
        (orr,) = _retention(z, decay16, layer=l, row0=n_ctx, n_seq=bl, t=tl, s0=state_ret,
                            y_prev=orr)
        cargs = (conv_w[l], conv_b[l][None], conv_ln_g[l][None], conv_ln_b[l][None])
        oc = _conv(z, *cargs, row0=0, n_seq=bc, t=tc, tt=math.gcd(512, tc))
        oc = _conv(z, *cargs, row0=n_ctx, n_seq=bl, t=tl, tt=math.gcd(512, tl), y_prev=oc)

        merged = _merge(oa, orr, oc, z, w_branch[l].astype(BF16), tm=tm_small)

        w_router = jnp.zeros((d, LANES), F32).at[:, :N_EXPERTS].set(w_router_expert[l])
        w_router = w_router.at[:, N_EXPERTS:N_EXPERTS + N_GROUPS].set(w_router_group[l])
        b_router = jnp.zeros((1, LANES), F32).at[0, :N_EXPERTS].set(b_router_expert[l])
        b_router = b_router.at[0, N_EXPERTS:N_EXPERTS + N_GROUPS].set(b_router_group[l])
        x, h2, comb, cnt = _outproj(merged, x, w_out[l].astype(BF16), mods3, g_norm2[l][None],
                                    jnp.stack(_split_bf16(w_router, 2)), b_router, set_of_row,
                                    tm=tm_small)
        base, c0, nc = _moe_plan(cnt)
        xs, cs = _moe_sort(h2, comb, base)
        ys = _moe_experts(xs, cs, c0, nc, w_exp_gate[l].astype(BF16), w_exp_up[l].astype(BF16),
                          w_exp_down[l].astype(BF16))
        if l + 1 < depth:
            x = _moe_unsort(ys, comb, x, mods3, base, set_of_row, row0=0, n_rows=n_ctx + n_lat)
        else:
            yp = _moe_unsort(ys, comb, x, mods3, base, set_of_row, row0=0, n_rows=n_ctx)
            ys_out = _moe_unsort(ys, comb, x, mods3, base, set_of_row, row0=n_ctx, n_rows=n_lat)

    return (yp.reshape(bc, tc, d), ys_out.reshape(bl, tl, d),
            new_kv[0].reshape(bc, depth, tc, N_HEADS, 2, SUB_W),
            new_kv[1].reshape(bc, depth, tc, N_HEADS, HEAD_W), new_st)
```

```python
import functools
import math

import jax
import jax.numpy as jnp
from jax import lax
from jax.experimental import pallas as pl
from jax.experimental.pallas import tpu as pltpu

F32 = jnp.float32
BF16 = jnp.bfloat16

EPS = 1e-6
LOG2E = 1.4426950408889634
GRID_W = 64
ROPE_BASE = 10000.0
HEAD_W = 128
SUB_W = 64
N_HEADS = 8
ATTN_KEY_BLOCK = 512
RET_HEADS_PER_STEP = 2
CHUNK = 128
CONV_K = 31
CONV_HALO = 16
N_EXPERTS = 16
EXP_PER_GROUP = 4
N_GROUPS = 4
LANES = 128
NEG = -1e30
VMEM_LIMIT = 58 * 1024 * 1024

MOE_SUB = 256
MOE_TILE = 1024
MOE_CHUNK = 128
MOE_ROWS = MOE_TILE + N_GROUPS * MOE_CHUNK
SORT_WIN = MOE_SUB + 16
SORT_WIN_SMALL = MOE_SUB // 2
UNSORT_WIN = MOE_SUB + LANES
UNSORT_WIN_SMALL = MOE_SUB


def _cparams(sem):
    return pltpu.CompilerParams(dimension_semantics=sem, vmem_limit_bytes=VMEM_LIMIT)


def _lane(shape):
    return lax.broadcasted_iota(jnp.int32, shape, len(shape) - 1)


def _sigmoid(x):
    return 1.0 / (1.0 + jnp.exp(-x))


def _silu(x):
    return x * _sigmoid(x)


def _dot(a, b):
    return jnp.dot(a, b, preferred_element_type=F32)


def _dot_nt(a, b):
    return lax.dot_general(a, b, (((1,), (1,)), ((), ())), preferred_element_type=F32)


def _split_bf16(x, parts):
    out = []
    for _ in range(parts - 1):
        hi = x.astype(BF16)
        out.append(hi)
        x = x - hi.astype(F32)
    out.append(x.astype(BF16))
    return out


_ANY = pl.BlockSpec(memory_space=pl.ANY)


def _mods_kernel(c_ref, w_ref, b_ref, o_ref):
    s = _silu(c_ref[...])
    o_ref[0] = jnp.dot(s, w_ref[0], preferred_element_type=F32,
                       precision=lax.Precision.HIGHEST) + b_ref[0]


def _mods(cond8, w_mod, b_mod):
    depth, d, n = w_mod.shape
    tn = 1024
    return pl.pallas_call(
        _mods_kernel,
        out_shape=jax.ShapeDtypeStruct((depth, 8, n), F32),
        grid=(depth, n // tn),
        in_specs=[pl.BlockSpec((8, d), lambda l, j: (0, 0)),
                  pl.BlockSpec((1, d, tn), lambda l, j: (l, 0, j)),
                  pl.BlockSpec((1, 1, tn), lambda l, j: (l, 0, j))],
        out_specs=pl.BlockSpec((1, 8, tn), lambda l, j: (l, 0, j)),
        compiler_params=_cparams(("parallel", "parallel")),
        name="ada_mods",
    )(cond8, w_mod, b_mod.reshape(depth, 1, n))


def _inproj_kernel(x_ref, g_ref, sh_ref, sc_ref, w_ref, o_ref, h_scr):
    @pl.when(pl.program_id(1) == 0)
    def _():
        x = x_ref[...]
        r = lax.rsqrt(jnp.mean(x * x, axis=-1, keepdims=True) + EPS)
        h = x * r * g_ref[...]
        h_scr[...] = (h * (1.0 + sc_ref[0]) + sh_ref[0]).astype(BF16)

    o_ref[...] = _dot(h_scr[...], w_ref[...]).astype(o_ref.dtype)


def _inproj(x, g, mods3, w_bf, layer, set_of_row, tm=1024, tn=1024):
    nt, d = x.shape
    n = w_bf.shape[2]
    return pl.pallas_call(
        _inproj_kernel,
        out_shape=jax.ShapeDtypeStruct((nt, n), BF16),
        grid=(nt // tm, n // tn),
        in_specs=[pl.BlockSpec((tm, d), lambda i, j: (i, 0)),
                  pl.BlockSpec((1, d), lambda i, j: (0, 0)),
                  pl.BlockSpec((1, 1, d), lambda i, j: (set_of_row(i * tm), 0, 0)),
                  pl.BlockSpec((1, 1, d), lambda i, j: (set_of_row(i * tm), 0, 1)),
                  pl.BlockSpec((None, d, tn), lambda i, j: (layer, 0, j))],
        out_specs=pl.BlockSpec((tm, tn), lambda i, j: (i, j)),
        scratch_shapes=[pltpu.VMEM((tm, d), BF16)],
        compiler_params=_cparams(("parallel", "arbitrary")),
        name="inproj",
    )(x, g, mods3, mods3, w_bf)


def _qk_norm(x, g):
    lo = _lane(x.shape) < SUB_W
    x2 = x * x
    s_lo = jnp.sum(jnp.where(lo, x2, 0.0), axis=-1, keepdims=True)
    s_hi = jnp.sum(jnp.where(lo, 0.0, x2), axis=-1, keepdims=True)
    r = jnp.where(lo, lax.rsqrt(s_lo / SUB_W + EPS), lax.rsqrt(s_hi / SUB_W + EPS))
    return x * r * g


def _rope(x, cos, sin_signed):
    first = (_lane(x.shape) % 32) < 16
    partner = jnp.where(first, pltpu.roll(x, LANES - 16, 1), pltpu.roll(x, 16, 1))
    return x * cos + partner * sin_signed


def _attn_kernel(*refs, tq, n_cache, rope, emit_kv, n_alias, lam_init):
    it = iter(refs)
    lam_ref, gq_ref, gk_ref, gs_ref = next(it), next(it), next(it), next(it)
    q_ref, k_ref, v_ref = next(it), next(it), next(it)
    if n_cache:
        ck_ref, cv_ref = next(it), next(it)
    if rope:
        cos_ref, sin_ref = next(it), next(it)
    for _ in range(n_alias):
        next(it)
    o_ref = next(it)
    if emit_kv:
        ak_ref, av_ref = next(it), next(it)
    kn_scr, v_scr, s_scr = next(it), next(it), next(it)

    qi = pl.program_id(2)

    @pl.when(qi == 0)
    def _():
        kn = _qk_norm(k_ref[...].astype(F32), gk_ref[...])
        if emit_kv:
            ak_ref[...] = kn
            av_ref[...] = v_ref[...].astype(F32)
        if rope:
            kn = _rope(kn, cos_ref[...], sin_ref[...])
        if n_cache:
            kn_scr[:n_cache, :] = ck_ref[...].astype(BF16)
            v_scr[:n_cache, :HEAD_W] = cv_ref[...].astype(BF16)
        kn_scr[n_cache:, :] = kn.astype(BF16)
        v_scr[n_cache:, :HEAD_W] = v_ref[...]
        v_scr[:, HEAD_W:] = jnp.ones((v_scr.shape[0], HEAD_W), BF16)

    lp = lam_ref[...]
    lam = (jnp.exp(jnp.sum(lp[0:1] * lp[1:2], axis=-1, keepdims=True))
           - jnp.exp(jnp.sum(lp[2:3] * lp[3:4], axis=-1, keepdims=True)) + lam_init)

    qn = _qk_norm(q_ref[...].astype(F32), gq_ref[...])
    if rope:
        r0 = pl.multiple_of(qi * tq, tq)
        qn = _rope(qn, cos_ref[pl.ds(r0, tq), :], sin_ref[pl.ds(r0, tq), :])
    qn = qn * (SUB_W ** -0.5 * LOG2E)
    lo = _lane(qn.shape) < SUB_W
    qs = (jnp.where(lo, qn, 0.0).astype(BF16), jnp.where(lo, 0.0, qn).astype(BF16))
    tk = kn_scr.shape[0]
    kb = math.gcd(tk, ATTN_KEY_BLOCK)
    nkb = tk // kb

    mx = [None, None]
    for j in range(nkb):
        kj = kn_scr[j * kb:(j + 1) * kb, :]
        for m in range(2):
            s = _dot_nt(qs[m], kj)
            s_scr[m, :, j * kb:(j + 1) * kb] = s
            for i in range(kb // LANES):
                blk = s[:, i * LANES:(i + 1) * LANES]
                mx[m] = blk if mx[m] is None else jnp.maximum(mx[m], blk)
    m_row = [jnp.max(mx[m], axis=-1, keepdims=True) for m in range(2)]

    acc = [jnp.zeros((tq, 2 * HEAD_W), F32), jnp.zeros((tq, 2 * HEAD_W), F32)]
    for j in range(nkb):
        vj = v_scr[j * kb:(j + 1) * kb, :]
        for m in range(2):
            p = jnp.exp2(s_scr[m, :, j * kb:(j + 1) * kb] - m_row[m])
            acc[m] = acc[m] + _dot(p.astype(BF16), vj)
    o = (acc[0][:, :HEAD_W] / acc[0][:, HEAD_W:]
         - lam * (acc[1][:, :HEAD_W] / acc[1][:, HEAD_W:]))
    o = o * lax.rsqrt(jnp.mean(o * o, axis=-1, keepdims=True) + EPS) * (gs_ref[...] * (1.0 - lam_init))
    o_ref[...] = o.astype(o_ref.dtype)


def _attention(z, lam_p, gq, gk, gs, *, layer, row0, n_seq, t, tq, lam_init, cache=None,
               rope_tabs=None, oa_prev=None, kv_prev=None, kv_shape=None):
    n_cache = 0 if cache is None else cache[0].shape[2]
    emit_kv = kv_shape is not None
    nq = t // tq
    rb = row0 // t
    rq = row0 // tq
    kcol, vcol = N_HEADS, 2 * N_HEADS
    in_specs = [pl.BlockSpec((4, SUB_W), lambda b, h, i: (0, 0)),
                pl.BlockSpec((1, HEAD_W), lambda b, h, i: (0, 0)),
                pl.BlockSpec((1, HEAD_W), lambda b, h, i: (0, 0)),
                pl.BlockSpec((1, HEAD_W), lambda b, h, i: (0, 0)),
                pl.BlockSpec((tq, HEAD_W), lambda b, h, i: (rq + b * nq + i, h)),
                pl.BlockSpec((t, HEAD_W), lambda b, h, i: (rb + b, kcol + h)),
                pl.BlockSpec((t, HEAD_W), lambda b, h, i: (rb + b, vcol + h))]
    args = [lam_p, gq, gk, gs, z, z, z]
    if n_cache:
        cspec = pl.BlockSpec((None, None, n_cache, HEAD_W), lambda b, h, i: (b, layer, 0, h))
        in_specs += [cspec, cspec]
        args += [cache[0], cache[1]]
    if rope_tabs is not None:
        in_specs += [pl.BlockSpec((t, HEAD_W), lambda b, h, i: (0, 0)),
                     pl.BlockSpec((t, HEAD_W), lambda b, h, i: (0, 0))]
        args += list(rope_tabs)
    aliases = {}
    n_alias = 0
    if oa_prev is not None:
        aliases[len(args)] = 0
        in_specs.append(_ANY)
        args.append(oa_prev)
        n_alias += 1
    if kv_prev is not None:
        for j, a in enumerate(kv_prev):
            aliases[len(args)] = 1 + j
            in_specs.append(_ANY)
            args.append(a)
            n_alias += 1
    out_shape = [jax.ShapeDtypeStruct((z.shape[0], N_HEADS * HEAD_W), BF16)]
    out_specs = [pl.BlockSpec((tq, HEAD_W), lambda b, h, i: (rq + b * nq + i, h))]
    if emit_kv:
        out_shape += [jax.ShapeDtypeStruct(kv_shape, F32)] * 2
        out_specs += [pl.BlockSpec((None, None, t, HEAD_W), lambda b, h, i: (b, layer, 0, h))] * 2
    return pl.pallas_call(
        functools.partial(_attn_kernel, tq=tq, n_cache=n_cache, rope=rope_tabs is not None,
                          emit_kv=emit_kv, n_alias=n_alias, lam_init=lam_init),
        out_shape=out_shape,
        grid=(n_seq, N_HEADS, nq),
        in_specs=in_specs,
        out_specs=out_specs,
        scratch_shapes=[pltpu.VMEM((n_cache + t, HEAD_W), BF16),
                        pltpu.VMEM((n_cache + t, 2 * HEAD_W), BF16),
                        pltpu.VMEM((2, tq, n_cache + t), F32)],
        input_output_aliases=aliases,
        compiler_params=_cparams(("parallel", "parallel", "arbitrary")),
        name="diff_attn_lat" if n_cache else "diff_attn_ctx",
    )(*args)


def _log_sigmoid(x):
    return jnp.minimum(x, 0.0) - jnp.log1p(jnp.exp(-jnp.abs(x)))


def _group_norm(o):
    mu = jnp.mean(o, axis=-1, keepdims=True)
    d = o - mu
    return d * lax.rsqrt(jnp.mean(d * d, axis=-1, keepdims=True) + EPS)


def _ret_kernel(*refs, n_chunks, has_s0, emit_state, n_alias):
    it = iter(refs)
    dlf_ref, dlb_ref = next(it), next(it)
    q_ref, k_ref, v_ref, gf_ref, gb_ref = (next(it) for _ in range(5))
    if has_s0:
        s0f_ref, s0b_ref = next(it), next(it)
    for _ in range(n_alias):
        next(it)
    y_ref = next(it)
    if emit_state:
        st_ref = next(it)
    of_scr, ob_scr, s_scr, tab_scr = next(it), next(it), next(it), next(it)

    c = CHUNK
    nh = RET_HEADS_PER_STEP
    row = lax.broadcasted_iota(jnp.int32, (c, c), 0).astype(F32)
    col = lax.broadcasted_iota(jnp.int32, (c, c), 1).astype(F32)
    diff = row - col
    k_scale = HEAD_W ** -0.5
    cdec = []
    for hh in range(nh):
        lgf = _log_sigmoid(dlf_ref[hh])
        lgb = _log_sigmoid(dlb_ref[hh])
        tab_scr[hh, 0, 0] = jnp.where(diff >= 0, jnp.exp(jnp.maximum(diff, 0.0) * lgf), 0.0) * k_scale
        tab_scr[hh, 0, 1] = jnp.exp((row + 1.0) * lgf)
        tab_scr[hh, 0, 2] = jnp.exp((c - 1.0 - row) * lgf) * k_scale
        tab_scr[hh, 1, 0] = jnp.where(diff <= 0, jnp.exp(jnp.maximum(-diff, 0.0) * lgb), 0.0) * k_scale
        tab_scr[hh, 1, 1] = jnp.exp((c - row) * lgb)
        tab_scr[hh, 1, 2] = jnp.exp(row * lgb) * k_scale
        cdec.append((jnp.exp(c * lgf), jnp.exp(c * lgb)))
        for dr in range(2):
            if has_s0:
                s_scr[hh, dr] = (s0f_ref, s0b_ref)[dr][hh]
            else:
                s_scr[hh, dr] = jnp.zeros((c, c), F32)

    chains = [(hh, dr) for hh in range(nh) for dr in range(2)]

    def step(i, carry):
        rows = [pl.multiple_of((i if dr == 0 else n_chunks - 1 - i) * c, c) for _, dr in chains]
        cols = [slice(hh * HEAD_W, (hh + 1) * HEAD_W) for hh, _ in chains]
        qc = [q_ref[pl.ds(r, c), hs] for r, hs in zip(rows, cols)]
        kc = [k_ref[pl.ds(r, c), hs] for r, hs in zip(rows, cols)]
        vc = [v_ref[pl.ds(r, c), hs] for r, hs in zip(rows, cols)]
        st = [s_scr[hh, dr] for hh, dr in chains]
        att = [(_dot_nt(q, k) * tab_scr[hh, dr, 0]).astype(BF16)
               for q, k, (hh, dr) in zip(qc, kc, chains)]
        cross = [_dot(q, s.astype(BF16)) * tab_scr[hh, dr, 1] for q, s, (hh, dr) in zip(qc, st, chains)]
        kd = [(k.astype(F32) * tab_scr[hh, dr, 2]).T.astype(BF16) for k, (hh, dr) in zip(kc, chains)]
        for n, (hh, dr) in enumerate(chains):
            (of_scr, ob_scr)[dr][pl.ds(rows[n], c), cols[n]] = _dot(att[n], vc[n]) + cross[n]
        for n, (hh, dr) in enumerate(chains):
            s_scr[hh, dr] = st[n] * cdec[hh][dr] + _dot(kd[n], vc[n])
        return carry

    lax.fori_loop(0, n_chunks, step, 0)

    def combine(ci, carry):
        r = pl.multiple_of(ci * c, c)
        for hh in range(nh):
            hs = slice(hh * HEAD_W, (hh + 1) * HEAD_W)
            y = (_group_norm(of_scr[pl.ds(r, c), hs]) * _silu(gf_ref[pl.ds(r, c), hs].astype(F32))
                 + _group_norm(ob_scr[pl.ds(r, c), hs]) * _silu(gb_ref[pl.ds(r, c), hs].astype(F32)))
            y_ref[pl.ds(r, c), hs] = y.astype(y_ref.dtype)
        return carry

    lax.fori_loop(0, n_chunks, combine, 0)
    if emit_state:
        for hh in range(nh):
            for dr in range(2):
                st_ref[dr, hh] = s_scr[hh, dr]


def _retention(z, decay16, *, layer, row0, n_seq, t, s0=None, y_prev=None, st_prev=None,
               st_shape=None):
    nh = RET_HEADS_PER_STEP
    hw = nh * HEAD_W
    rb = row0 // t
    hp = N_HEADS // nh
    emit_state = st_shape is not None
    col = lambda j: (lambda b, h: (rb + b, j * hp + h))
    in_specs = [pl.BlockSpec((nh, 1, 1), lambda b, h: (h, 0, 0)),
                pl.BlockSpec((nh, 1, 1), lambda b, h: (hp + h, 0, 0))]
    in_specs += [pl.BlockSpec((t, hw), col(j)) for j in (3, 4, 5, 6, 7)]
    args = [decay16, decay16, z, z, z, z, z]
    if s0 is not None:
        for dr in range(2):
            in_specs.append(pl.BlockSpec((None, None, None, nh, CHUNK, CHUNK),
                                         lambda b, h, dr=dr: (b, layer, dr, h, 0, 0)))
        args += [s0, s0]
    aliases = {}
    n_alias = 0
    if y_prev is not None:
        aliases[len(args)] = 0
        in_specs.append(_ANY)
        args.append(y_prev)
        n_alias += 1
    if st_prev is not None:
        aliases[len(args)] = 1
        in_specs.append(_ANY)
        args.append(st_prev)
        n_alias += 1
    out_shape = [jax.ShapeDtypeStruct((z.shape[0], N_HEADS * HEAD_W), BF16)]
    out_specs = [pl.BlockSpec((t, hw), lambda b, h: (rb + b, h))]
    if emit_state:
        out_shape.append(jax.ShapeDtypeStruct(st_shape, F32))
        out_specs.append(pl.BlockSpec((None, None, 2, nh, CHUNK, CHUNK),
                                      lambda b, h: (b, layer, 0, h, 0, 0)))
    return pl.pallas_call(
        functools.partial(_ret_kernel, n_chunks=t // CHUNK, has_s0=s0 is not None,
                          emit_state=emit_state, n_alias=n_alias),
        out_shape=out_shape,
        grid=(n_seq, hp),
        in_specs=in_specs,
        out_specs=out_specs,
        scratch_shapes=[pltpu.VMEM((t, hw), F32), pltpu.VMEM((t, hw), F32),
                        pltpu.VMEM((nh, 2, CHUNK, CHUNK), F32),
                        pltpu.VMEM((nh, 2, 3, CHUNK, CHUNK), F32)],
        input_output_aliases=aliases,
        compiler_params=_cparams(("parallel", "parallel")),
        name="retention_lat" if s0 is not None else "retention_ctx",
    )(*args)


def _conv_kernel(*refs, tt, nt, rblk, n_alias):
    (a_ref, g_ref, ap_ref, gp_ref, an_ref, gn_ref, w_ref, b_ref, lg_ref, lb_ref) = refs[:10]
    o_ref, u_scr, sh_scr, y_scr = refs[10 + n_alias:]
    ti = pl.program_id(1)
    halo = CONV_HALO
    cw = a_ref.shape[1]

    def glu(a, g):
        return a[...].astype(F32) * _sigmoid(g[...].astype(F32))

    u_scr[halo:halo + tt, :] = glu(a_ref, g_ref)
    u_scr[0:halo, :] = jnp.where(ti > 0, glu(ap_ref, gp_ref), 0.0)
    u_scr[halo + tt:, :] = jnp.where(ti < nt - 1, glu(an_ref, gn_ref), 0.0)

    ext = tt + 2 * halo
    for cb in range(cw // LANES):
        cs = slice(cb * LANES, (cb + 1) * LANES)
        x = u_scr[:, cs]
        sh_scr[0] = x
        for b in range(1, 8):
            sh_scr[b] = pltpu.roll(x, ext - b, 0)

        def rows(ri, carry):
            r = pl.multiple_of(ri * rblk, rblk)
            acc = jnp.zeros((rblk, LANES), F32)
            for k in range(CONV_K):
                off = halo - CONV_K // 2 + k
                acc = acc + sh_scr[off % 8, pl.ds(r + 8 * (off // 8), rblk), :] * w_ref[k:k + 1, cs]
            y_scr[pl.ds(r, rblk), cs] = acc + b_ref[:, cs]
            return carry

        lax.fori_loop(0, tt // rblk, rows, 0)

    y = y_scr[...]
    mu = jnp.mean(y, axis=-1, keepdims=True)
    d = y - mu
    yn = d * lax.rsqrt(jnp.mean(d * d, axis=-1, keepdims=True) + EPS) * lg_ref[...] + lb_ref[...]
    o_ref[...] = _silu(yn).astype(o_ref.dtype)


def _conv(z, w, b, lg, lb, *, row0, n_seq, t, tt, y_prev=None):
    cw = w.shape[1]
    nt = t // tt
    total_h = z.shape[0] // CONV_HALO
    acol = 8192 // cw
    r_t = row0 // tt
    r_h = row0 // CONV_HALO
    per_h = tt // CONV_HALO

    def cur(j):
        return lambda s, i: (r_t + s * nt + i, acol + j)

    def prev(j):
        return lambda s, i: (jnp.maximum(r_h + (s * nt + i) * per_h - 1, 0), acol + j)

    def nxt(j):
        return lambda s, i: (jnp.minimum(r_h + (s * nt + i + 1) * per_h, total_h - 1), acol + j)

    full = lambda s, i: (0, 0)
    in_specs = [pl.BlockSpec((tt, cw), cur(0)), pl.BlockSpec((tt, cw), cur(1)),
                pl.BlockSpec((CONV_HALO, cw), prev(0)), pl.BlockSpec((CONV_HALO, cw), prev(1)),
                pl.BlockSpec((CONV_HALO, cw), nxt(0)), pl.BlockSpec((CONV_HALO, cw), nxt(1)),
                pl.BlockSpec((CONV_K, cw), full), pl.BlockSpec((1, cw), full),
                pl.BlockSpec((1, cw), full), pl.BlockSpec((1, cw), full)]
    args = [z, z, z, z, z, z, w, b, lg, lb]
    aliases = {}
    if y_prev is not None:
        aliases[len(args)] = 0
        in_specs.append(_ANY)
        args.append(y_prev)
    return pl.pallas_call(
        functools.partial(_conv_kernel, tt=tt, nt=nt, rblk=64, n_alias=len(aliases)),
        out_shape=jax.ShapeDtypeStruct((z.shape[0], cw), BF16),
        grid=(n_seq, nt),
        in_specs=in_specs,
        out_specs=pl.BlockSpec((tt, cw), lambda s, i: (r_t + s * nt + i, 0)),
        scratch_shapes=[pltpu.VMEM((tt + 2 * CONV_HALO, cw), F32),
                        pltpu.VMEM((8, tt + 2 * CONV_HALO, LANES), F32),
                        pltpu.VMEM((tt, cw), F32)],
        input_output_aliases=aliases,
        compiler_params=_cparams(("parallel", "parallel")),
        name="conformer_conv",
    )(*args)


def _merge_kernel(a_ref, r_ref, c_ref, g0_ref, g1_ref, g2_ref, w_ref, o_ref):
    acc = _sigmoid(g0_ref[...].astype(F32)) * _dot(a_ref[...], w_ref[0])
    acc += _sigmoid(g1_ref[...].astype(F32)) * _dot(r_ref[...], w_ref[1])
    acc += _sigmoid(g2_ref[...].astype(F32)) * _dot(c_ref[...], w_ref[2])
    o_ref[...] = acc.astype(o_ref.dtype)


def _merge(oa, orr, oc, z, w_br, layer, tm=512, tn=1024):
    nt, bw = oa.shape
    d = w_br.shape[3]
    g0 = 10240 // tn
    gs = d // tn
    br = pl.BlockSpec((tm, bw), lambda i, j: (i, 0))
    return pl.pallas_call(
        _merge_kernel,
        out_shape=jax.ShapeDtypeStruct((nt, d), BF16),
        grid=(nt // tm, d // tn),
        in_specs=[br, br, br,
                  pl.BlockSpec((tm, tn), lambda i, j: (i, g0 + j)),
                  pl.BlockSpec((tm, tn), lambda i, j: (i, g0 + gs + j)),
                  pl.BlockSpec((tm, tn), lambda i, j: (i, g0 + 2 * gs + j)),
                  pl.BlockSpec((None, 3, bw, tn), lambda i, j: (layer, 0, 0, j))],
        out_specs=pl.BlockSpec((tm, tn), lambda i, j: (i, j)),
        compiler_params=_cparams(("parallel", "parallel")),
        name="branch_merge",
    )(oa, orr, oc, z, z, z, w_br)


def _outproj_kernel(m_ref, x_ref, w_ref, g1_ref, gn_ref, sh_ref, sc_ref, wr_ref, br_ref,
                    xo_ref, h_ref, cb_ref, cnt_ref):
    x = x_ref[...] + g1_ref[0] * _dot(m_ref[...], w_ref[...])
    xo_ref[...] = x
    h = x * lax.rsqrt(jnp.mean(x * x, axis=-1, keepdims=True) + EPS) * gn_ref[...]
    h = h * (1.0 + sc_ref[0]) + sh_ref[0]
    h_hi = h.astype(BF16)
    h_ref[...] = h_hi
    h_lo = (h - h_hi.astype(F32)).astype(BF16)

    lg = _dot(h_hi, wr_ref[0]) + _dot(h_hi, wr_ref[1]) + _dot(h_lo, wr_ref[0]) + br_ref[...]
    lane = _lane(lg.shape).astype(F32)
    big = float(LANES)

    def first_max(v):
        m = jnp.max(v, axis=-1, keepdims=True)
        return m, jnp.min(jnp.where(v == m, lane, big), axis=-1, keepdims=True)

    is_g = (lane >= N_EXPERTS) & (lane < N_EXPERTS + N_GROUPS)
    gl = jnp.where(is_g, lg, NEG)
    gmax, gidx = first_max(gl)
    g_w = 1.0 / jnp.sum(jnp.exp(gl - gmax), axis=-1, keepdims=True)
    e_lo = (gidx - N_EXPERTS) * EXP_PER_GROUP
    is_e = (lane >= e_lo) & (lane < e_lo + EXP_PER_GROUP)
    el = jnp.where(is_e, lg, NEG)
    e1, i1 = first_max(el)
    e2, i2 = first_max(jnp.where(lane == i1, NEG, el))
    p2 = jnp.exp(e2 - e1)
    w1 = g_w / (1.0 + p2)
    onehot = jnp.where(lane == gidx, 1.0, 0.0)
    cb_ref[...] = jnp.where(lane == i1, w1, 0.0) + jnp.where(lane == i2, w1 * p2, 0.0) + onehot
    for s in range(cnt_ref.shape[0]):
        cnt_ref[s] = jnp.sum(onehot[s * MOE_SUB:(s + 1) * MOE_SUB], axis=0, keepdims=True)


def _outproj(merged, x, w_o, layer, mods3, g2, w_router, b_router, set_of_row, tm=512):
    nt, d = x.shape
    mod = lambda c: pl.BlockSpec((1, 1, d), lambda i: (set_of_row(i * tm), 0, c))
    full = lambda i: (0, 0)
    ns = tm // MOE_SUB
    return pl.pallas_call(
        _outproj_kernel,
        out_shape=[jax.ShapeDtypeStruct((nt, d), F32),
                   jax.ShapeDtypeStruct((nt, d), BF16),
                   jax.ShapeDtypeStruct((nt, LANES), F32),
                   jax.ShapeDtypeStruct((nt // MOE_SUB, 1, LANES), F32)],
        grid=(nt // tm,),
        in_specs=[pl.BlockSpec((tm, d), lambda i: (i, 0)),
                  pl.BlockSpec((tm, d), lambda i: (i, 0)),
                  pl.BlockSpec((None, d, d), lambda i: (layer, 0, 0)),
                  mod(2), pl.BlockSpec((1, d), full), mod(3), mod(4),
                  pl.BlockSpec((2, d, LANES), lambda i: (0, 0, 0)), pl.BlockSpec((1, LANES), full)],
        out_specs=[pl.BlockSpec((tm, d), lambda i: (i, 0)),
                   pl.BlockSpec((tm, d), lambda i: (i, 0)),
                   pl.BlockSpec((tm, LANES), lambda i: (i, 0)),
                   pl.BlockSpec((ns, 1, LANES), lambda i: (i, 0, 0))],
        compiler_params=_cparams(("parallel",)),
        name="outproj_router",
    )(merged, x, w_o, mods3, g2, mods3, mods3, w_router, b_router)


def _group_onehot(cb):
    lane = _lane(cb.shape)
    return jnp.where((lane >= N_EXPERTS) & (lane < N_EXPERTS + N_GROUPS), cb, 0.0)


def _window_start(base, win):
    w0 = jnp.minimum((base // 16) * 16, MOE_ROWS - win)
    return pl.multiple_of(w0, 16)


def _moe_sort_kernel(base_ref, len_ref, h_ref, cb_ref, xs_ref, cs_ref, acc_scr):
    i, s = pl.program_id(0), pl.program_id(1)
    n_sub = pl.num_programs(1)

    @pl.when(s == 0)
    def _():
        acc_scr[...] = jnp.zeros_like(acc_scr)
        cs_ref[...] = jnp.zeros_like(cs_ref)

    cb = cb_ref[...]
    g1h = _group_onehot(cb).astype(BF16)
    eye = (lax.broadcasted_iota(jnp.int32, (LANES, LANES), 0)
           == lax.broadcasted_iota(jnp.int32, (LANES, LANES), 1)).astype(BF16)
    gt = _dot_nt(eye, g1h)
    before = (lax.broadcasted_iota(jnp.int32, (MOE_SUB, MOE_SUB), 0)
              < lax.broadcasted_iota(jnp.int32, (MOE_SUB, MOE_SUB), 1)).astype(BF16)
    rank_t = _dot(gt.astype(BF16), before)
    sub = lax.broadcasted_iota(jnp.int32, (LANES, 1), 0)
    bases = [base_ref[(i * n_sub + s) * N_GROUPS + g] for g in range(N_GROUPS)]
    basec = jnp.zeros((LANES, 1), F32)
    for g in range(N_GROUPS):
        basec = jnp.where(sub == N_EXPERTS + g, bases[g].astype(F32), basec)
    dest = jnp.sum(gt * (basec + rank_t), axis=0, keepdims=True)
    cparts = _split_bf16(cb, 3)
    lens = [len_ref[(i * n_sub + s) * N_GROUPS + g] for g in range(N_GROUPS)]
    longest = functools.reduce(jnp.maximum, lens)

    def place(win):
        w0 = [_window_start(bases[g], win) for g in range(N_GROUPS)]
        perm = []
        for g in range(N_GROUPS):
            slot = lax.broadcasted_iota(jnp.int32, (win, MOE_SUB), 0).astype(F32) + w0[g].astype(F32)
            ing = gt[N_EXPERTS + g:N_EXPERTS + g + 1, :]
            perm.append(jnp.where((slot == dest) & (ing > 0.5), 1.0, 0.0).astype(BF16))
        perm = jnp.concatenate(perm, axis=0)
        xg = _dot(perm, h_ref[...])
        cg = _dot(perm, cparts[0]) + _dot(perm, cparts[1]) + _dot(perm, cparts[2])
        for g in range(N_GROUPS):
            acc_scr[pl.ds(w0[g], win), :] += xg[g * win:(g + 1) * win]
            cs_ref[pl.ds(w0[g], win), :] += cg[g * win:(g + 1) * win]

    @pl.when(longest <= SORT_WIN_SMALL - 16)
    def _():
        place(SORT_WIN_SMALL)

    @pl.when(longest > SORT_WIN_SMALL - 16)
    def _():
        place(SORT_WIN)

    @pl.when(s == n_sub - 1)
    def _():
        xs_ref[...] = acc_scr[...].astype(BF16)


def _moe_expert_kernel(c0_ref, nc_ref, xs_ref, cs_ref, wg_ref, wu_ref, wd_ref, ys_ref):
    i, e = pl.program_id(0), pl.program_id(1)
    g = e // EXP_PER_GROUP

    @pl.when(e == 0)
    def _():
        ys_ref[...] = jnp.zeros_like(ys_ref)

    c0 = c0_ref[i * N_GROUPS + g]
    nc = nc_ref[i * N_GROUPS + g]

    def run(ci, rows):
        r = pl.multiple_of(ci * MOE_CHUNK, MOE_CHUNK)
        x = xs_ref[pl.ds(r, rows), :]
        cw = cs_ref[pl.ds(r, rows), :]
        w_e = jnp.sum(jnp.where(_lane(cw.shape) == e, cw, 0.0), axis=-1, keepdims=True)
        act = _silu(_dot(x, wg_ref[0])) * _dot(x, wu_ref[0]) * w_e
        ys_ref[pl.ds(r, rows), :] += _dot(act.astype(BF16), wd_ref[0])

    def pair(k, carry):
        run(c0 + 2 * k, 2 * MOE_CHUNK)
        return carry

    lax.fori_loop(0, nc // 2, pair, 0)

    @pl.when(nc % 2 == 1)
    def _():
        run(c0 + nc - 1, MOE_CHUNK)


def _moe_unsort_kernel(base_ref, len_ref, ys_ref, cb_ref, x_ref, g2_ref, o_ref, *, tile0):
    i, s = pl.program_id(0), pl.program_id(2)
    n_sub = pl.num_programs(2)
    cb = cb_ref[...]
    g1h = _group_onehot(cb)
    after = (lax.broadcasted_iota(jnp.int32, (MOE_SUB, MOE_SUB), 1)
             < lax.broadcasted_iota(jnp.int32, (MOE_SUB, MOE_SUB), 0)).astype(BF16)
    rank = _dot(after, g1h.astype(BF16))
    lane = _lane((1, LANES))
    bases = [base_ref[((tile0 + i) * n_sub + s) * N_GROUPS + g] for g in range(N_GROUPS)]
    basev = jnp.zeros((1, LANES), F32)
    for g in range(N_GROUPS):
        basev = jnp.where(lane == N_EXPERTS + g, bases[g].astype(F32), basev)
    dest = jnp.sum(g1h * (basev + rank), axis=-1, keepdims=True)
    lens = [len_ref[((tile0 + i) * n_sub + s) * N_GROUPS + g] for g in range(N_GROUPS)]
    longest = functools.reduce(jnp.maximum, lens)

    def gather(win):
        acc = jnp.zeros(o_ref.shape, F32)
        for g in range(N_GROUPS):
            w0 = _window_start(bases[g], win)
            slot = lax.broadcasted_iota(jnp.int32, (MOE_SUB, win), 1).astype(F32) + w0.astype(F32)
            ing = jnp.sum(jnp.where(_lane(cb.shape) == N_EXPERTS + g, cb, 0.0), axis=-1, keepdims=True)
            perm_t = jnp.where((slot == dest) & (ing > 0.5), 1.0, 0.0).astype(BF16)
            acc += _dot(perm_t, ys_ref[pl.ds(w0, win), :].astype(BF16))
        o_ref[...] = x_ref[...] + g2_ref[0] * acc

    @pl.when(longest <= UNSORT_WIN_SMALL - 16)
    def _():
        gather(UNSORT_WIN_SMALL)

    @pl.when(longest > UNSORT_WIN_SMALL - 16)
    def _():
        gather(UNSORT_WIN)


def _moe_sort(h2, comb, base, lens):
    nt, d = h2.shape
    n_tiles, n_sub = nt // MOE_TILE, MOE_TILE // MOE_SUB
    return pl.pallas_call(
        _moe_sort_kernel,
        out_shape=[jax.ShapeDtypeStruct((n_tiles * MOE_ROWS, d), BF16),
                   jax.ShapeDtypeStruct((n_tiles * MOE_ROWS, LANES), F32)],
        grid_spec=pltpu.PrefetchScalarGridSpec(
            num_scalar_prefetch=2,
            grid=(n_tiles, n_sub),
            in_specs=[pl.BlockSpec((MOE_SUB, d), lambda i, s, b, n: (i * n_sub + s, 0)),
                      pl.BlockSpec((MOE_SUB, LANES), lambda i, s, b, n: (i * n_sub + s, 0))],
            out_specs=[pl.BlockSpec((MOE_ROWS, d), lambda i, s, b, n: (i, 0)),
                       pl.BlockSpec((MOE_ROWS, LANES), lambda i, s, b, n: (i, 0))],
            scratch_shapes=[pltpu.VMEM((MOE_ROWS, d), F32)]),
        compiler_params=_cparams(("parallel", "arbitrary")),
        name="moe_sort",
    )(base, lens, h2, comb)


def _moe_experts(xs, cs, c0, nc, wg, wu, wd, layer):
    d = xs.shape[1]
    n_tiles = xs.shape[0] // MOE_ROWS
    _, ne, _, ff = wg.shape
    return pl.pallas_call(
        _moe_expert_kernel,
        out_shape=jax.ShapeDtypeStruct((n_tiles * MOE_ROWS, d), F32),
        grid_spec=pltpu.PrefetchScalarGridSpec(
            num_scalar_prefetch=2,
            grid=(n_tiles, ne),
            in_specs=[pl.BlockSpec((MOE_ROWS, d), lambda i, e, a, b: (i, 0)),
                      pl.BlockSpec((MOE_ROWS, LANES), lambda i, e, a, b: (i, 0)),
                      pl.BlockSpec((None, 1, d, ff), lambda i, e, a, b: (layer, e, 0, 0)),
                      pl.BlockSpec((None, 1, d, ff), lambda i, e, a, b: (layer, e, 0, 0)),
                      pl.BlockSpec((None, 1, ff, d), lambda i, e, a, b: (layer, e, 0, 0))],
            out_specs=pl.BlockSpec((MOE_ROWS, d), lambda i, e, a, b: (i, 0))),
        compiler_params=_cparams(("parallel", "arbitrary")),
        name="moe_experts",
    )(c0, nc, xs, cs, wg, wu, wd)


def _moe_unsort(ys, comb, x, mods3, base, lens, set_of_row, *, row0, n_rows, tn=1024):
    d = x.shape[1]
    n_sub = MOE_TILE // MOE_SUB
    tile0, sub0 = row0 // MOE_TILE, row0 // MOE_SUB
    return pl.pallas_call(
        functools.partial(_moe_unsort_kernel, tile0=tile0),
        out_shape=jax.ShapeDtypeStruct((n_rows, d), F32),
        grid_spec=pltpu.PrefetchScalarGridSpec(
            num_scalar_prefetch=2,
            grid=(n_rows // MOE_TILE, d // tn, n_sub),
            in_specs=[pl.BlockSpec((MOE_ROWS, tn), lambda i, j, s, b, n: (tile0 + i, j)),
                      pl.BlockSpec((MOE_SUB, LANES), lambda i, j, s, b, n: (sub0 + i * n_sub + s, 0)),
                      pl.BlockSpec((MOE_SUB, tn), lambda i, j, s, b, n: (sub0 + i * n_sub + s, j)),
                      pl.BlockSpec((1, 1, tn),
                                   lambda i, j, s, b, n: (set_of_row(row0 + i * MOE_TILE), 0,
                                                          5 * (d // tn) + j))],
            out_specs=pl.BlockSpec((MOE_SUB, tn), lambda i, j, s, b, n: (i * n_sub + s, j))),
        compiler_params=_cparams(("parallel", "parallel", "arbitrary")),
        name="moe_unsort",
    )(base, lens, ys, comb, x, mods3)


def _moe_plan(cnt):
    n_sub = MOE_TILE // MOE_SUB
    c = cnt[:, 0, N_EXPERTS:N_EXPERTS + N_GROUPS].astype(jnp.int32).reshape(-1, n_sub, N_GROUPS)
    seg = (jnp.sum(c, axis=1) + MOE_CHUNK - 1) // MOE_CHUNK * MOE_CHUNK
    seg_start = jnp.cumsum(seg, axis=1) - seg
    base = seg_start[:, None, :] + jnp.cumsum(c, axis=1) - c
    return (base.reshape(-1), c.reshape(-1), (seg_start // MOE_CHUNK).reshape(-1),
            (seg // MOE_CHUNK).reshape(-1))


def _rope_tables(t):
    n_freq = SUB_W // 4
    tok = jnp.arange(t)
    row = (tok // GRID_W).astype(F32)
    col = (tok % GRID_W).astype(F32)
    inv = ROPE_BASE ** (-jnp.arange(n_freq, dtype=F32) / n_freq)
    ar, ac = row[:, None] * inv, col[:, None] * inv
    cos = jnp.concatenate([jnp.cos(ar)] * 2 + [jnp.cos(ac)] * 2, axis=1)
    sin = jnp.concatenate([-jnp.sin(ar), jnp.sin(ar), -jnp.sin(ac), jnp.sin(ac)], axis=1)
    return jnp.tile(cos, (1, 2)), jnp.tile(sin, (1, 2))


def kernel(x_prompt, x_sample, cache_diff_k, cache_diff_v, state_ret, c, c_ctx, w_mod, b_mod,
           g_norm1, g_norm2, w_in, g_qnorm, g_knorm, lambda_qk, g_subln, ret_decay_logit, conv_w,
           conv_b, conv_ln_g, conv_ln_b, w_branch, w_out, w_router_group, b_router_group,
           w_router_expert, b_router_expert, w_exp_gate, w_exp_up, w_exp_down):
    bc, tc, d = x_prompt.shape
    bl, tl, _ = x_sample.shape
    depth = w_mod.shape[0]
    past = cache_diff_k.shape[2]
    n_ctx, n_lat = bc * tc, bl * tl
    assert bl + 1 <= 8 and n_ctx % tl == 0 and n_ctx % MOE_TILE == 0 and tl % MOE_TILE == 0

    def set_of_row(r):
        return jnp.where(r < n_ctx, 0, 1 + (r - n_ctx) // tl)

    tm_big = math.gcd(1024, math.gcd(n_ctx, tl))
    tm_small = math.gcd(512, tm_big)

    x = jnp.concatenate([x_prompt.reshape(n_ctx, d), x_sample.reshape(n_lat, d)], axis=0)
    cond8 = jnp.zeros((8, d), F32).at[0].set(c_ctx).at[1:1 + bl].set(c)
    mods = _mods(cond8, w_mod, b_mod)
    cos, sin = _rope_tables(tl)
    ck = cache_diff_k.reshape(bl, depth, past, N_HEADS * HEAD_W)
    cv = cache_diff_v.reshape(bl, depth, past, N_HEADS * HEAD_W)
    kv_shape = (bc, depth, tc, N_HEADS * HEAD_W)
    st_shape = (bc, depth, 2, N_HEADS, CHUNK, CHUNK)

    w_in_bf, w_br_bf, w_out_bf = w_in.astype(BF16), w_branch.astype(BF16), w_out.astype(BF16)
    wg_bf, wu_bf, wd_bf = w_exp_gate.astype(BF16), w_exp_up.astype(BF16), w_exp_down.astype(BF16)

    new_kv, new_st = None, None
    for l in range(depth):
        lam_init = 0.8 - 0.6 * math.exp(-0.3 * l)
        mods3 = mods[l].reshape(8, 1, 6 * d)
        gq = jnp.tile(g_qnorm[l], 2)[None]
        gk = jnp.tile(g_knorm[l], 2)[None]
        gs = g_subln[l][None]
        decay16 = ret_decay_logit[l].reshape(2 * N_HEADS, 1, 1)

        z = _inproj(x, g_norm1[l][None], mods3, w_in_bf, l, set_of_row, tm=tm_big)

        oa, ak, av = _attention(z, lambda_qk[l], gq, gk, gs, layer=l, row0=0, n_seq=bc, t=tc,
                                tq=tc, lam_init=lam_init, kv_prev=new_kv, kv_shape=kv_shape)
        new_kv = (ak, av)
        (oa,) = _attention(z, lambda_qk[l], gq, gk, gs, layer=l, row0=n_ctx, n_seq=bl, t=tl,
                           tq=math.gcd(256, tl), lam_init=lam_init, cache=(ck, cv),
                           rope_tabs=(cos, sin), oa_prev=oa)
        orr, new_st = _retention(z, decay16, layer=l, row0=0, n_seq=bc, t=tc, st_prev=new_st,
                                 st_shape=st_shape)
        (orr,) = _retention(z, decay16, layer=l, row0=n_ctx, n_seq=bl, t=tl, s0=state_ret,
                            y_prev=orr)
        cargs = (conv_w[l], conv_b[l][None], conv_ln_g[l][None], conv_ln_b[l][None])
        oc = _conv(z, *cargs, row0=0, n_seq=bc, t=tc, tt=math.gcd(512, tc))
        oc = _conv(z, *cargs, row0=n_ctx, n_seq=bl, t=tl, tt=math.gcd(512, tl), y_prev=oc)

        merged = _merge(oa, orr, oc, z, w_br_bf, l, tm=tm_small)

        w_router = jnp.zeros((d, LANES), F32).at[:, :N_EXPERTS].set(w_router_expert[l])
        w_router = w_router.at[:, N_EXPERTS:N_EXPERTS + N_GROUPS].set(w_router_group[l])
        b_router = jnp.zeros((1, LANES), F32).at[0, :N_EXPERTS].set(b_router_expert[l])
        b_router = b_router.at[0, N_EXPERTS:N_EXPERTS + N_GROUPS].set(b_router_group[l])
        x, h2, comb, cnt = _outproj(merged, x, w_out_bf, l, mods3, g_norm2[l][None],
                                    jnp.stack(_split_bf16(w_router, 2)), b_router, set_of_row,
                                    tm=tm_small)
        base, lens, c0, nc = _moe_plan(cnt)
        xs, cs = _moe_sort(h2, comb, base, lens)
        ys = _moe_experts(xs, cs, c0, nc, wg_bf, wu_bf, wd_bf, l)
        unsort = functools.partial(_moe_unsort, ys, comb, x, mods3, base, lens, set_of_row)
        if l + 1 < depth:
            x = unsort(row0=0, n_rows=n_ctx + n_lat)
        else:
            yp = unsort(row0=0, n_rows=n_ctx)
            ys_out = unsort(row0=n_ctx, n_rows=n_lat)

    return (yp.reshape(bc, tc, d), ys_out.reshape(bl, tl, d),
            new_kv[0].reshape(bc, depth, tc, N_HEADS, 2, SUB_W),
            new_kv[1].reshape(bc, depth, tc, N_HEADS, HEAD_W), new_st)
```

```python
import functools
import math

import jax
import jax.numpy as jnp
from jax import lax
from jax.experimental import pallas as pl
from jax.experimental.pallas import tpu as pltpu

F32 = jnp.float32
BF16 = jnp.bfloat16

EPS = 1e-6
LOG2E = 1.4426950408889634
GRID_W = 64
ROPE_BASE = 10000.0
HEAD_W = 128
SUB_W = 64
N_HEADS = 8
ATTN_KEY_BLOCK = 512
RET_HEADS_PER_STEP = 2
CHUNK = 128
CONV_K = 31
CONV_HALO = 16
N_EXPERTS = 16
EXP_PER_GROUP = 4
N_GROUPS = 4
LANES = 128
NEG = -1e30
VMEM_LIMIT = 58 * 1024 * 1024

MOE_SUB = 256
MOE_TILE = 2048
MOE_CHUNK = 128
MOE_ROWS = MOE_TILE + N_GROUPS * MOE_CHUNK
SORT_WIN = MOE_SUB + 16
SORT_WIN_SMALL = MOE_SUB // 2
UNSORT_WIN = MOE_SUB + LANES
UNSORT_WIN_SMALL = MOE_SUB


def _cparams(sem):
    return pltpu.CompilerParams(dimension_semantics=sem, vmem_limit_bytes=VMEM_LIMIT)


def _lane(shape):
    return lax.broadcasted_iota(jnp.int32, shape, len(shape) - 1)


def _sigmoid(x):
    return 1.0 / (1.0 + jnp.exp(-x))


def _silu(x):
    return x * _sigmoid(x)


def _dot(a, b):
    return jnp.dot(a, b, preferred_element_type=F32)


def _dot_nt(a, b):
    return lax.dot_general(a, b, (((1,), (1,)), ((), ())), preferred_element_type=F32)


def _split_bf16(x, parts):
    out = []
    for _ in range(parts - 1):
        hi = x.astype(BF16)
        out.append(hi)
        x = x - hi.astype(F32)
    out.append(x.astype(BF16))
    return out


_ANY = pl.BlockSpec(memory_space=pl.ANY)
_ONCE = pl.Buffered(1)


def _mods_kernel(c_ref, w_ref, b_ref, o_ref):
    s = _silu(c_ref[...])
    o_ref[0] = jnp.dot(s, w_ref[0], preferred_element_type=F32,
                       precision=lax.Precision.HIGHEST) + b_ref[0]


def _mods(cond8, w_mod, b_mod):
    depth, d, n = w_mod.shape
    tn = 1024
    return pl.pallas_call(
        _mods_kernel,
        out_shape=jax.ShapeDtypeStruct((depth, 8, n), F32),
        grid=(depth, n // tn),
        in_specs=[pl.BlockSpec((8, d), lambda l, j: (0, 0)),
                  pl.BlockSpec((1, d, tn), lambda l, j: (l, 0, j)),
                  pl.BlockSpec((1, 1, tn), lambda l, j: (l, 0, j))],
        out_specs=pl.BlockSpec((1, 8, tn), lambda l, j: (l, 0, j)),
        compiler_params=_cparams(("parallel", "parallel")),
        name="ada_mods",
    )(cond8, w_mod, b_mod.reshape(depth, 1, n))


def _inproj_kernel(x_ref, g_ref, sh_ref, sc_ref, w_ref, o_ref, h_scr):
    @pl.when(pl.program_id(1) == 0)
    def _():
        x = x_ref[...]
        r = lax.rsqrt(jnp.mean(x * x, axis=-1, keepdims=True) + EPS)
        h = x * r * g_ref[...]
        h_scr[...] = (h * (1.0 + sc_ref[0]) + sh_ref[0]).astype(BF16)

    o_ref[...] = _dot(h_scr[...], w_ref[...]).astype(o_ref.dtype)


def _inproj(x, g, mods3, w_bf, layer, set_of_row, tm=1024, tn=1024):
    nt, d = x.shape
    n = w_bf.shape[2]
    return pl.pallas_call(
        _inproj_kernel,
        out_shape=jax.ShapeDtypeStruct((nt, n), BF16),
        grid=(nt // tm, n // tn),
        in_specs=[pl.BlockSpec((tm, d), lambda i, j: (i, 0)),
                  pl.BlockSpec((1, d), lambda i, j: (0, 0)),
                  pl.BlockSpec((1, 1, d), lambda i, j: (set_of_row(i * tm), 0, 0)),
                  pl.BlockSpec((1, 1, d), lambda i, j: (set_of_row(i * tm), 0, 1)),
                  pl.BlockSpec((None, d, tn), lambda i, j: (layer, 0, j))],
        out_specs=pl.BlockSpec((tm, tn), lambda i, j: (i, j)),
        scratch_shapes=[pltpu.VMEM((tm, d), BF16)],
        compiler_params=_cparams(("parallel", "arbitrary")),
        name="inproj",
    )(x, g, mods3, mods3, w_bf)


def _qk_norm(x, g):
    lo = _lane(x.shape) < SUB_W
    x2 = x * x
    s_lo = jnp.sum(jnp.where(lo, x2, 0.0), axis=-1, keepdims=True)
    s_hi = jnp.sum(jnp.where(lo, 0.0, x2), axis=-1, keepdims=True)
    r = jnp.where(lo, lax.rsqrt(s_lo / SUB_W + EPS), lax.rsqrt(s_hi / SUB_W + EPS))
    return x * r * g


def _rope(x, cos, sin_signed):
    first = (_lane(x.shape) % 32) < 16
    partner = jnp.where(first, pltpu.roll(x, LANES - 16, 1), pltpu.roll(x, 16, 1))
    return x * cos + partner * sin_signed


def _attn_kernel(*refs, tq, n_cache, rope, emit_kv, n_alias, lam_init):
    it = iter(refs)
    lam_ref, gq_ref, gk_ref, gs_ref = next(it), next(it), next(it), next(it)
    q_ref, k_ref, v_ref = next(it), next(it), next(it)
    if n_cache:
        ck_ref, cv_ref = next(it), next(it)
    if rope:
        cos_ref, sin_ref = next(it), next(it)
    for _ in range(n_alias):
        next(it)
    o_ref = next(it)
    if emit_kv:
        ak_ref, av_ref = next(it), next(it)
    kn_scr, v_scr, s_scr = next(it), next(it), next(it)

    qi = pl.program_id(2)

    @pl.when(qi == 0)
    def _():
        kn = _qk_norm(k_ref[...].astype(F32), gk_ref[...])
        if emit_kv:
            ak_ref[...] = kn
            av_ref[...] = v_ref[...].astype(F32)
        if rope:
            kn = _rope(kn, cos_ref[...], sin_ref[...])
        if n_cache:
            kn_scr[:n_cache, :] = ck_ref[...].astype(BF16)
            v_scr[:n_cache, :HEAD_W] = cv_ref[...].astype(BF16)
        kn_scr[n_cache:, :] = kn.astype(BF16)
        v_scr[n_cache:, :HEAD_W] = v_ref[...]
        v_scr[:, HEAD_W:] = jnp.ones((v_scr.shape[0], HEAD_W), BF16)

    lp = lam_ref[...]
    lam = (jnp.exp(jnp.sum(lp[0:1] * lp[1:2], axis=-1, keepdims=True))
           - jnp.exp(jnp.sum(lp[2:3] * lp[3:4], axis=-1, keepdims=True)) + lam_init)

    qn = _qk_norm(q_ref[...].astype(F32), gq_ref[...])
    if rope:
        r0 = pl.multiple_of(qi * tq, tq)
        qn = _rope(qn, cos_ref[pl.ds(r0, tq), :], sin_ref[pl.ds(r0, tq), :])
    qn = qn * (SUB_W ** -0.5 * LOG2E)
    lo = _lane(qn.shape) < SUB_W
    qs = (jnp.where(lo, qn, 0.0).astype(BF16), jnp.where(lo, 0.0, qn).astype(BF16))
    tk = kn_scr.shape[0]
    kb = math.gcd(tk, ATTN_KEY_BLOCK)
    nkb = tk // kb

    mx = [None, None]
    for j in range(nkb):
        kj = kn_scr[j * kb:(j + 1) * kb, :]
        for m in range(2):
            s = _dot_nt(qs[m], kj)
            s_scr[m, :, j * kb:(j + 1) * kb] = s
            for i in range(kb // LANES):
                blk = s[:, i * LANES:(i + 1) * LANES]
                mx[m] = blk if mx[m] is None else jnp.maximum(mx[m], blk)
    m_row = [jnp.max(mx[m], axis=-1, keepdims=True) for m in range(2)]

    acc = [jnp.zeros((tq, 2 * HEAD_W), F32), jnp.zeros((tq, 2 * HEAD_W), F32)]
    for j in range(nkb):
        vj = v_scr[j * kb:(j + 1) * kb, :]
        for m in range(2):
            p = jnp.exp2(s_scr[m, :, j * kb:(j + 1) * kb] - m_row[m])
            acc[m] = acc[m] + _dot(p.astype(BF16), vj)
    o = (acc[0][:, :HEAD_W] / acc[0][:, HEAD_W:]
         - lam * (acc[1][:, :HEAD_W] / acc[1][:, HEAD_W:]))
    o = o * lax.rsqrt(jnp.mean(o * o, axis=-1, keepdims=True) + EPS) * (gs_ref[...] * (1.0 - lam_init))
    o_ref[...] = o.astype(o_ref.dtype)


def _attention(z, lam_p, gq, gk, gs, *, layer, row0, n_seq, t, tq, lam_init, cache=None,
               rope_tabs=None, oa_prev=None, kv_prev=None, kv_shape=None):
    n_cache = 0 if cache is None else cache[0].shape[2]
    emit_kv = kv_shape is not None
    nq = t // tq
    rb = row0 // t
    rq = row0 // tq
    kcol, vcol = N_HEADS, 2 * N_HEADS
    in_specs = [pl.BlockSpec((4, SUB_W), lambda b, h, i: (0, 0)),
                pl.BlockSpec((1, HEAD_W), lambda b, h, i: (0, 0)),
                pl.BlockSpec((1, HEAD_W), lambda b, h, i: (0, 0)),
                pl.BlockSpec((1, HEAD_W), lambda b, h, i: (0, 0)),
                pl.BlockSpec((tq, HEAD_W), lambda b, h, i: (rq + b * nq + i, h)),
                pl.BlockSpec((t, HEAD_W), lambda b, h, i: (rb + b, kcol + h)),
                pl.BlockSpec((t, HEAD_W), lambda b, h, i: (rb + b, vcol + h))]
    args = [lam_p, gq, gk, gs, z, z, z]
    if n_cache:
        cspec = pl.BlockSpec((None, None, n_cache, HEAD_W), lambda b, h, i: (b, layer, 0, h))
        in_specs += [cspec, cspec]
        args += [cache[0], cache[1]]
    if rope_tabs is not None:
        in_specs += [pl.BlockSpec((t, HEAD_W), lambda b, h, i: (0, 0)),
                     pl.BlockSpec((t, HEAD_W), lambda b, h, i: (0, 0))]
        args += list(rope_tabs)
    aliases = {}
    n_alias = 0
    if oa_prev is not None:
        aliases[len(args)] = 0
        in_specs.append(_ANY)
        args.append(oa_prev)
        n_alias += 1
    if kv_prev is not None:
        for j, a in enumerate(kv_prev):
            aliases[len(args)] = 1 + j
            in_specs.append(_ANY)
            args.append(a)
            n_alias += 1
    out_shape = [jax.ShapeDtypeStruct((z.shape[0], N_HEADS * HEAD_W), BF16)]
    out_specs = [pl.BlockSpec((tq, HEAD_W), lambda b, h, i: (rq + b * nq + i, h))]
    if emit_kv:
        out_shape += [jax.ShapeDtypeStruct(kv_shape, F32)] * 2
        out_specs += [pl.BlockSpec((None, None, t, HEAD_W), lambda b, h, i: (b, layer, 0, h))] * 2
    return pl.pallas_call(
        functools.partial(_attn_kernel, tq=tq, n_cache=n_cache, rope=rope_tabs is not None,
                          emit_kv=emit_kv, n_alias=n_alias, lam_init=lam_init),
        out_shape=out_shape,
        grid=(n_seq, N_HEADS, nq),
        in_specs=in_specs,
        out_specs=out_specs,
        scratch_shapes=[pltpu.VMEM((n_cache + t, HEAD_W), BF16),
                        pltpu.VMEM((n_cache + t, 2 * HEAD_W), BF16),
                        pltpu.VMEM((2, tq, n_cache + t), F32)],
        input_output_aliases=aliases,
        compiler_params=_cparams(("parallel", "parallel", "arbitrary")),
        name="diff_attn_lat" if n_cache else "diff_attn_ctx",
    )(*args)


def _log_sigmoid(x):
    return jnp.minimum(x, 0.0) - jnp.log1p(jnp.exp(-jnp.abs(x)))


def _group_norm(o):
    mu = jnp.mean(o, axis=-1, keepdims=True)
    d = o - mu
    return d * lax.rsqrt(jnp.mean(d * d, axis=-1, keepdims=True) + EPS)


def _ret_kernel(*refs, n_chunks, has_s0, emit_state, n_alias):
    it = iter(refs)
    dlf_ref, dlb_ref = next(it), next(it)
    q_ref, k_ref, v_ref, gf_ref, gb_ref = (next(it) for _ in range(5))
    if has_s0:
        s0f_ref, s0b_ref = next(it), next(it)
    for _ in range(n_alias):
        next(it)
    y_ref = next(it)
    if emit_state:
        st_ref = next(it)
    of_scr, ob_scr, s_scr, tab_scr = next(it), next(it), next(it), next(it)

    c = CHUNK
    nh = RET_HEADS_PER_STEP
    row = lax.broadcasted_iota(jnp.int32, (c, c), 0).astype(F32)
    col = lax.broadcasted_iota(jnp.int32, (c, c), 1).astype(F32)
    diff = row - col
    k_scale = HEAD_W ** -0.5
    cdec = []
    for hh in range(nh):
        lgf = _log_sigmoid(dlf_ref[hh])
        lgb = _log_sigmoid(dlb_ref[hh])
        tab_scr[hh, 0, 0] = jnp.where(diff >= 0, jnp.exp(jnp.maximum(diff, 0.0) * lgf), 0.0) * k_scale
        tab_scr[hh, 0, 1] = jnp.exp((row + 1.0) * lgf)
        tab_scr[hh, 0, 2] = jnp.exp((c - 1.0 - row) * lgf) * k_scale
        tab_scr[hh, 1, 0] = jnp.where(diff <= 0, jnp.exp(jnp.maximum(-diff, 0.0) * lgb), 0.0) * k_scale
        tab_scr[hh, 1, 1] = jnp.exp((c - row) * lgb)
        tab_scr[hh, 1, 2] = jnp.exp(row * lgb) * k_scale
        cdec.append((jnp.exp(c * lgf), jnp.exp(c * lgb)))
        for dr in range(2):
            if has_s0:
                s_scr[hh, dr] = (s0f_ref, s0b_ref)[dr][hh]
            else:
                s_scr[hh, dr] = jnp.zeros((c, c), F32)

    chains = [(hh, dr) for hh in range(nh) for dr in range(2)]

    def step(i, carry):
        rows = [pl.multiple_of((i if dr == 0 else n_chunks - 1 - i) * c, c) for _, dr in chains]
        cols = [slice(hh * HEAD_W, (hh + 1) * HEAD_W) for hh, _ in chains]
        qc = [q_ref[pl.ds(r, c), hs] for r, hs in zip(rows, cols)]
        kc = [k_ref[pl.ds(r, c), hs] for r, hs in zip(rows, cols)]
        vc = [v_ref[pl.ds(r, c), hs] for r, hs in zip(rows, cols)]
        st = [s_scr[hh, dr] for hh, dr in chains]
        att = [(_dot_nt(q, k) * tab_scr[hh, dr, 0]).astype(BF16)
               for q, k, (hh, dr) in zip(qc, kc, chains)]
        cross = [_dot(q, s.astype(BF16)) * tab_scr[hh, dr, 1] for q, s, (hh, dr) in zip(qc, st, chains)]
        kd = [(k.astype(F32) * tab_scr[hh, dr, 2]).T.astype(BF16) for k, (hh, dr) in zip(kc, chains)]
        for n, (hh, dr) in enumerate(chains):
            (of_scr, ob_scr)[dr][pl.ds(rows[n], c), cols[n]] = _dot(att[n], vc[n]) + cross[n]
        for n, (hh, dr) in enumerate(chains):
            s_scr[hh, dr] = st[n] * cdec[hh][dr] + _dot(kd[n], vc[n])
        return carry

    lax.fori_loop(0, n_chunks, step, 0)

    def combine(ci, carry):
        r = pl.multiple_of(ci * c, c)
        for hh in range(nh):
            hs = slice(hh * HEAD_W, (hh + 1) * HEAD_W)
            y = (_group_norm(of_scr[pl.ds(r, c), hs]) * _silu(gf_ref[pl.ds(r, c), hs].astype(F32))
                 + _group_norm(ob_scr[pl.ds(r, c), hs]) * _silu(gb_ref[pl.ds(r, c), hs].astype(F32)))
            y_ref[pl.ds(r, c), hs] = y.astype(y_ref.dtype)
        return carry

    lax.fori_loop(0, n_chunks, combine, 0)
    if emit_state:
        for hh in range(nh):
            for dr in range(2):
                st_ref[dr, hh] = s_scr[hh, dr]


def _retention(z, decay16, *, layer, row0, n_seq, t, s0=None, y_prev=None, st_prev=None,
               st_shape=None):
    nh = RET_HEADS_PER_STEP
    hw = nh * HEAD_W
    rb = row0 // t
    hp = N_HEADS // nh
    emit_state = st_shape is not None
    col = lambda j: (lambda b, h: (rb + b, j * hp + h))
    in_specs = [pl.BlockSpec((nh, 1, 1), lambda b, h: (h, 0, 0)),
                pl.BlockSpec((nh, 1, 1), lambda b, h: (hp + h, 0, 0))]
    in_specs += [pl.BlockSpec((t, hw), col(j)) for j in (3, 4, 5, 6, 7)]
    args = [decay16, decay16, z, z, z, z, z]
    if s0 is not None:
        for dr in range(2):
            in_specs.append(pl.BlockSpec((None, None, None, nh, CHUNK, CHUNK),
                                         lambda b, h, dr=dr: (b, layer, dr, h, 0, 0)))
        args += [s0, s0]
    aliases = {}
    n_alias = 0
    if y_prev is not None:
        aliases[len(args)] = 0
        in_specs.append(_ANY)
        args.append(y_prev)
        n_alias += 1
    if st_prev is not None:
        aliases[len(args)] = 1
        in_specs.append(_ANY)
        args.append(st_prev)
        n_alias += 1
    out_shape = [jax.ShapeDtypeStruct((z.shape[0], N_HEADS * HEAD_W), BF16)]
    out_specs = [pl.BlockSpec((t, hw), lambda b, h: (rb + b, h))]
    if emit_state:
        out_shape.append(jax.ShapeDtypeStruct(st_shape, F32))
        out_specs.append(pl.BlockSpec((None, None, 2, nh, CHUNK, CHUNK),
                                      lambda b, h: (b, layer, 0, h, 0, 0)))
    return pl.pallas_call(
        functools.partial(_ret_kernel, n_chunks=t // CHUNK, has_s0=s0 is not None,
                          emit_state=emit_state, n_alias=n_alias),
        out_shape=out_shape,
        grid=(n_seq, hp),
        in_specs=in_specs,
        out_specs=out_specs,
        scratch_shapes=[pltpu.VMEM((t, hw), F32), pltpu.VMEM((t, hw), F32),
                        pltpu.VMEM((nh, 2, CHUNK, CHUNK), F32),
                        pltpu.VMEM((nh, 2, 3, CHUNK, CHUNK), F32)],
        input_output_aliases=aliases,
        compiler_params=_cparams(("parallel", "parallel")),
        name="retention_lat" if s0 is not None else "retention_ctx",
    )(*args)


def _conv_kernel(*refs, tt, nt, rblk, n_alias):
    (a_ref, g_ref, ap_ref, gp_ref, an_ref, gn_ref, w_ref, b_ref, lg_ref, lb_ref) = refs[:10]
    o_ref, u_scr, sh_scr, y_scr = refs[10 + n_alias:]
    ti = pl.program_id(1)
    halo = CONV_HALO
    cw = a_ref.shape[1]

    def glu(a, g):
        return a[...].astype(F32) * _sigmoid(g[...].astype(F32))

    u_scr[halo:halo + tt, :] = glu(a_ref, g_ref)
    u_scr[0:halo, :] = jnp.where(ti > 0, glu(ap_ref, gp_ref), 0.0)
    u_scr[halo + tt:, :] = jnp.where(ti < nt - 1, glu(an_ref, gn_ref), 0.0)

    ext = tt + 2 * halo
    for cb in range(cw // LANES):
        cs = slice(cb * LANES, (cb + 1) * LANES)
        x = u_scr[:, cs]
        sh_scr[0] = x
        for b in range(1, 8):
            sh_scr[b] = pltpu.roll(x, ext - b, 0)

        def rows(ri, carry):
            r = pl.multiple_of(ri * rblk, rblk)
            acc = jnp.zeros((rblk, LANES), F32)
            for k in range(CONV_K):
                off = halo - CONV_K // 2 + k
                acc = acc + sh_scr[off % 8, pl.ds(r + 8 * (off // 8), rblk), :] * w_ref[k:k + 1, cs]
            y_scr[pl.ds(r, rblk), cs] = acc + b_ref[:, cs]
            return carry

        lax.fori_loop(0, tt // rblk, rows, 0)

    y = y_scr[...]
    mu = jnp.mean(y, axis=-1, keepdims=True)
    d = y - mu
    yn = d * lax.rsqrt(jnp.mean(d * d, axis=-1, keepdims=True) + EPS) * lg_ref[...] + lb_ref[...]
    o_ref[...] = _silu(yn).astype(o_ref.dtype)


def _conv(z, w, b, lg, lb, *, row0, n_seq, t, tt, y_prev=None):
    cw = w.shape[1]
    nt = t // tt
    total_h = z.shape[0] // CONV_HALO
    acol = 8192 // cw
    r_t = row0 // tt
    r_h = row0 // CONV_HALO
    per_h = tt // CONV_HALO

    def cur(j):
        return lambda s, i: (r_t + s * nt + i, acol + j)

    def prev(j):
        return lambda s, i: (jnp.maximum(r_h + (s * nt + i) * per_h - 1, 0), acol + j)

    def nxt(j):
        return lambda s, i: (jnp.minimum(r_h + (s * nt + i + 1) * per_h, total_h - 1), acol + j)

    full = lambda s, i: (0, 0)
    in_specs = [pl.BlockSpec((tt, cw), cur(0)), pl.BlockSpec((tt, cw), cur(1)),
                pl.BlockSpec((CONV_HALO, cw), prev(0)), pl.BlockSpec((CONV_HALO, cw), prev(1)),
                pl.BlockSpec((CONV_HALO, cw), nxt(0)), pl.BlockSpec((CONV_HALO, cw), nxt(1)),
                pl.BlockSpec((CONV_K, cw), full), pl.BlockSpec((1, cw), full),
                pl.BlockSpec((1, cw), full), pl.BlockSpec((1, cw), full)]
    args = [z, z, z, z, z, z, w, b, lg, lb]
    aliases = {}
    if y_prev is not None:
        aliases[len(args)] = 0
        in_specs.append(_ANY)
        args.append(y_prev)
    return pl.pallas_call(
        functools.partial(_conv_kernel, tt=tt, nt=nt, rblk=64, n_alias=len(aliases)),
        out_shape=jax.ShapeDtypeStruct((z.shape[0], cw), BF16),
        grid=(n_seq, nt),
        in_specs=in_specs,
        out_specs=pl.BlockSpec((tt, cw), lambda s, i: (r_t + s * nt + i, 0)),
        scratch_shapes=[pltpu.VMEM((tt + 2 * CONV_HALO, cw), F32),
                        pltpu.VMEM((8, tt + 2 * CONV_HALO, LANES), F32),
                        pltpu.VMEM((tt, cw), F32)],
        input_output_aliases=aliases,
        compiler_params=_cparams(("parallel", "parallel")),
        name="conformer_conv",
    )(*args)


def _merge_kernel(a_ref, r_ref, c_ref, g0_ref, g1_ref, g2_ref, w_ref, o_ref):
    acc = _sigmoid(g0_ref[...].astype(F32)) * _dot(a_ref[...], w_ref[0])
    acc += _sigmoid(g1_ref[...].astype(F32)) * _dot(r_ref[...], w_ref[1])
    acc += _sigmoid(g2_ref[...].astype(F32)) * _dot(c_ref[...], w_ref[2])
    o_ref[...] = acc.astype(o_ref.dtype)


def _merge(oa, orr, oc, z, w_br, layer, tm=512, tn=1024):
    nt, bw = oa.shape
    d = w_br.shape[3]
    g0 = 10240 // tn
    gs = d // tn
    br = pl.BlockSpec((tm, bw), lambda i, j: (i, 0))
    return pl.pallas_call(
        _merge_kernel,
        out_shape=jax.ShapeDtypeStruct((nt, d), BF16),
        grid=(nt // tm, d // tn),
        in_specs=[br, br, br,
                  pl.BlockSpec((tm, tn), lambda i, j: (i, g0 + j)),
                  pl.BlockSpec((tm, tn), lambda i, j: (i, g0 + gs + j)),
                  pl.BlockSpec((tm, tn), lambda i, j: (i, g0 + 2 * gs + j)),
                  pl.BlockSpec((None, 3, bw, tn), lambda i, j: (layer, 0, 0, j))],
        out_specs=pl.BlockSpec((tm, tn), lambda i, j: (i, j)),
        compiler_params=_cparams(("parallel", "parallel")),
        name="branch_merge",
    )(oa, orr, oc, z, z, z, w_br)


def _outproj_kernel(m_ref, x_ref, w_ref, g1_ref, gn_ref, sh_ref, sc_ref, wr_ref, br_ref,
                    xo_ref, h_ref, cb_ref, cnt_ref):
    x = x_ref[...] + g1_ref[0] * _dot(m_ref[...], w_ref[...])
    xo_ref[...] = x
    h = x * lax.rsqrt(jnp.mean(x * x, axis=-1, keepdims=True) + EPS) * gn_ref[...]
    h = h * (1.0 + sc_ref[0]) + sh_ref[0]
    h_hi = h.astype(BF16)
    h_ref[...] = h_hi
    h_lo = (h - h_hi.astype(F32)).astype(BF16)

    lg = _dot(h_hi, wr_ref[0]) + _dot(h_hi, wr_ref[1]) + _dot(h_lo, wr_ref[0]) + br_ref[...]
    lane = _lane(lg.shape).astype(F32)
    big = float(LANES)

    def first_max(v):
        m = jnp.max(v, axis=-1, keepdims=True)
        return m, jnp.min(jnp.where(v == m, lane, big), axis=-1, keepdims=True)

    is_g = (lane >= N_EXPERTS) & (lane < N_EXPERTS + N_GROUPS)
    gl = jnp.where(is_g, lg, NEG)
    gmax, gidx = first_max(gl)
    g_w = 1.0 / jnp.sum(jnp.exp(gl - gmax), axis=-1, keepdims=True)
    e_lo = (gidx - N_EXPERTS) * EXP_PER_GROUP
    is_e = (lane >= e_lo) & (lane < e_lo + EXP_PER_GROUP)
    el = jnp.where(is_e, lg, NEG)
    e1, i1 = first_max(el)
    e2, i2 = first_max(jnp.where(lane == i1, NEG, el))
    p2 = jnp.exp(e2 - e1)
    w1 = g_w / (1.0 + p2)
    onehot = jnp.where(lane == gidx, 1.0, 0.0)
    cb_ref[...] = jnp.where(lane == i1, w1, 0.0) + jnp.where(lane == i2, w1 * p2, 0.0) + onehot
    for s in range(cnt_ref.shape[0]):
        cnt_ref[s] = jnp.sum(onehot[s * MOE_SUB:(s + 1) * MOE_SUB], axis=0, keepdims=True)


def _outproj(merged, x, w_o, layer, mods3, g2, w_router, b_router, set_of_row, tm=512):
    nt, d = x.shape
    mod = lambda c: pl.BlockSpec((1, 1, d), lambda i: (set_of_row(i * tm), 0, c))
    full = lambda i: (0, 0)
    ns = tm // MOE_SUB
    return pl.pallas_call(
        _outproj_kernel,
        out_shape=[jax.ShapeDtypeStruct((nt, d), F32),
                   jax.ShapeDtypeStruct((nt, d), BF16),
                   jax.ShapeDtypeStruct((nt, LANES), F32),
                   jax.ShapeDtypeStruct((nt // MOE_SUB, 1, LANES), F32)],
        grid=(nt // tm,),
        in_specs=[pl.BlockSpec((tm, d), lambda i: (i, 0)),
                  pl.BlockSpec((tm, d), lambda i: (i, 0)),
                  pl.BlockSpec((None, d, d), lambda i: (layer, 0, 0)),
                  mod(2), pl.BlockSpec((1, d), full), mod(3), mod(4),
                  pl.BlockSpec((2, d, LANES), lambda i: (0, 0, 0)), pl.BlockSpec((1, LANES), full)],
        out_specs=[pl.BlockSpec((tm, d), lambda i: (i, 0)),
                   pl.BlockSpec((tm, d), lambda i: (i, 0)),
                   pl.BlockSpec((tm, LANES), lambda i: (i, 0)),
                   pl.BlockSpec((ns, 1, LANES), lambda i: (i, 0, 0))],
        compiler_params=_cparams(("parallel",)),
        name="outproj_router",
    )(merged, x, w_o, mods3, g2, mods3, mods3, w_router, b_router)


def _group_onehot(cb):
    lane = _lane(cb.shape)
    return jnp.where((lane >= N_EXPERTS) & (lane < N_EXPERTS + N_GROUPS), cb, 0.0)


def _window_start(base, win):
    w0 = jnp.minimum((base // 16) * 16, MOE_ROWS - win)
    return pl.multiple_of(w0, 16)


def _moe_sort_kernel(base_ref, len_ref, h_ref, cb_ref, xs_ref, cs_ref, acc_scr):
    i, s = pl.program_id(0), pl.program_id(1)
    n_sub = pl.num_programs(1)

    @pl.when(s == 0)
    def _():
        acc_scr[...] = jnp.zeros_like(acc_scr)
        cs_ref[...] = jnp.zeros_like(cs_ref)

    cb = cb_ref[...]
    g1h = _group_onehot(cb).astype(BF16)
    eye = (lax.broadcasted_iota(jnp.int32, (LANES, LANES), 0)
           == lax.broadcasted_iota(jnp.int32, (LANES, LANES), 1)).astype(BF16)
    gt = _dot_nt(eye, g1h)
    before = (lax.broadcasted_iota(jnp.int32, (MOE_SUB, MOE_SUB), 0)
              < lax.broadcasted_iota(jnp.int32, (MOE_SUB, MOE_SUB), 1)).astype(BF16)
    rank_t = _dot(gt.astype(BF16), before)
    sub = lax.broadcasted_iota(jnp.int32, (LANES, 1), 0)
    bases = [base_ref[(i * n_sub + s) * N_GROUPS + g] for g in range(N_GROUPS)]
    basec = jnp.zeros((LANES, 1), F32)
    for g in range(N_GROUPS):
        basec = jnp.where(sub == N_EXPERTS + g, bases[g].astype(F32), basec)
    dest = jnp.sum(gt * (basec + rank_t), axis=0, keepdims=True)
    cparts = _split_bf16(cb, 3)
    lens = [len_ref[(i * n_sub + s) * N_GROUPS + g] for g in range(N_GROUPS)]
    longest = functools.reduce(jnp.maximum, lens)

    def place(win):
        w0 = [_window_start(bases[g], win) for g in range(N_GROUPS)]
        perm = []
        for g in range(N_GROUPS):
            slot = lax.broadcasted_iota(jnp.int32, (win, MOE_SUB), 0).astype(F32) + w0[g].astype(F32)
            ing = gt[N_EXPERTS + g:N_EXPERTS + g + 1, :]
            perm.append(jnp.where((slot == dest) & (ing > 0.5), 1.0, 0.0).astype(BF16))
        perm = jnp.concatenate(perm, axis=0)
        xg = _dot(perm, h_ref[...])
        cg = _dot(perm, cparts[0]) + _dot(perm, cparts[1]) + _dot(perm, cparts[2])
        for g in range(N_GROUPS):
            acc_scr[pl.ds(w0[g], win), :] += xg[g * win:(g + 1) * win]
            cs_ref[pl.ds(w0[g], win), :] += cg[g * win:(g + 1) * win]

    @pl.when(longest <= SORT_WIN_SMALL - 16)
    def _():
        place(SORT_WIN_SMALL)

    @pl.when(longest > SORT_WIN_SMALL - 16)
    def _():
        place(SORT_WIN)

    @pl.when(s == n_sub - 1)
    def _():
        xs_ref[...] = acc_scr[...].astype(BF16)


def _moe_expert_kernel(c0_ref, nc_ref, xs_ref, cs_ref, wg_ref, wu_ref, wd_ref, ys_ref):
    i, e = pl.program_id(0), pl.program_id(1)
    g = e // EXP_PER_GROUP

    @pl.when(e == 0)
    def _():
        ys_ref[...] = jnp.zeros_like(ys_ref)

    c0 = c0_ref[i * N_GROUPS + g]
    nc = nc_ref[i * N_GROUPS + g]

    def run(ci, rows):
        r = pl.multiple_of(ci * MOE_CHUNK, MOE_CHUNK)
        x = xs_ref[pl.ds(r, rows), :]
        cw = cs_ref[pl.ds(r, rows), :]
        w_e = jnp.sum(jnp.where(_lane(cw.shape) == e, cw, 0.0), axis=-1, keepdims=True)
        act = _silu(_dot(x, wg_ref[0])) * _dot(x, wu_ref[0]) * w_e
        ys_ref[pl.ds(r, rows), :] += _dot(act.astype(BF16), wd_ref[0])

    def quad(k, carry):
        run(c0 + 4 * k, 4 * MOE_CHUNK)
        return carry

    lax.fori_loop(0, nc // 4, quad, 0)

    @pl.when(nc % 4 >= 2)
    def _():
        run(c0 + nc // 4 * 4, 2 * MOE_CHUNK)

    @pl.when(nc % 2 == 1)
    def _():
        run(c0 + nc - 1, MOE_CHUNK)


def _moe_unsort_kernel(base_ref, len_ref, ys_ref, cb_ref, x_ref, g2_ref, o_ref, *, tile0):
    i, s = pl.program_id(0), pl.program_id(2)
    n_sub = pl.num_programs(2)
    cb = cb_ref[...]
    g1h = _group_onehot(cb)
    after = (lax.broadcasted_iota(jnp.int32, (MOE_SUB, MOE_SUB), 1)
             < lax.broadcasted_iota(jnp.int32, (MOE_SUB, MOE_SUB), 0)).astype(BF16)
    rank = _dot(after, g1h.astype(BF16))
    lane = _lane((1, LANES))
    bases = [base_ref[((tile0 + i) * n_sub + s) * N_GROUPS + g] for g in range(N_GROUPS)]
    basev = jnp.zeros((1, LANES), F32)
    for g in range(N_GROUPS):
        basev = jnp.where(lane == N_EXPERTS + g, bases[g].astype(F32), basev)
    dest = jnp.sum(g1h * (basev + rank), axis=-1, keepdims=True)
    lens = [len_ref[((tile0 + i) * n_sub + s) * N_GROUPS + g] for g in range(N_GROUPS)]
    longest = functools.reduce(jnp.maximum, lens)

    def gather(win):
        acc = jnp.zeros(o_ref.shape, F32)
        for g in range(N_GROUPS):
            w0 = _window_start(bases[g], win)
            slot = lax.broadcasted_iota(jnp.int32, (MOE_SUB, win), 1).astype(F32) + w0.astype(F32)
            ing = jnp.sum(jnp.where(_lane(cb.shape) == N_EXPERTS + g, cb, 0.0), axis=-1, keepdims=True)
            perm_t = jnp.where((slot == dest) & (ing > 0.5), 1.0, 0.0).astype(BF16)
            acc += _dot(perm_t, ys_ref[pl.ds(w0, win), :].astype(BF16))
        o_ref[...] = x_ref[...] + g2_ref[0] * acc

    @pl.when(longest <= UNSORT_WIN_SMALL - 16)
    def _():
        gather(UNSORT_WIN_SMALL)

    @pl.when(longest > UNSORT_WIN_SMALL - 16)
    def _():
        gather(UNSORT_WIN)


def _moe_sort(h2, comb, base, lens):
    nt, d = h2.shape
    n_tiles, n_sub = nt // MOE_TILE, MOE_TILE // MOE_SUB
    return pl.pallas_call(
        _moe_sort_kernel,
        out_shape=[jax.ShapeDtypeStruct((n_tiles * MOE_ROWS, d), BF16),
                   jax.ShapeDtypeStruct((n_tiles * MOE_ROWS, LANES), F32)],
        grid_spec=pltpu.PrefetchScalarGridSpec(
            num_scalar_prefetch=2,
            grid=(n_tiles, n_sub),
            in_specs=[pl.BlockSpec((MOE_SUB, d), lambda i, s, b, n: (i * n_sub + s, 0)),
                      pl.BlockSpec((MOE_SUB, LANES), lambda i, s, b, n: (i * n_sub + s, 0))],
            out_specs=[pl.BlockSpec((MOE_ROWS, d), lambda i, s, b, n: (i, 0), pipeline_mode=_ONCE),
                       pl.BlockSpec((MOE_ROWS, LANES), lambda i, s, b, n: (i, 0), pipeline_mode=_ONCE)],
            scratch_shapes=[pltpu.VMEM((MOE_ROWS, d), F32)]),
        compiler_params=_cparams(("parallel", "arbitrary")),
        name="moe_sort",
    )(base, lens, h2, comb)


def _moe_experts(xs, cs, c0, nc, wg, wu, wd, layer):
    d = xs.shape[1]
    n_tiles = xs.shape[0] // MOE_ROWS
    _, ne, _, ff = wg.shape
    return pl.pallas_call(
        _moe_expert_kernel,
        out_shape=jax.ShapeDtypeStruct((n_tiles * MOE_ROWS, d), F32),
        grid_spec=pltpu.PrefetchScalarGridSpec(
            num_scalar_prefetch=2,
            grid=(n_tiles, ne),
            in_specs=[pl.BlockSpec((MOE_ROWS, d), lambda i, e, a, b: (i, 0), pipeline_mode=_ONCE),
                      pl.BlockSpec((MOE_ROWS, LANES), lambda i, e, a, b: (i, 0), pipeline_mode=_ONCE),
                      pl.BlockSpec((None, 1, d, ff), lambda i, e, a, b: (layer, e, 0, 0)),
                      pl.BlockSpec((None, 1, d, ff), lambda i, e, a, b: (layer, e, 0, 0)),
                      pl.BlockSpec((None, 1, ff, d), lambda i, e, a, b: (layer, e, 0, 0))],
            out_specs=pl.BlockSpec((MOE_ROWS, d), lambda i, e, a, b: (i, 0), pipeline_mode=_ONCE)),
        compiler_params=_cparams(("parallel", "arbitrary")),
        name="moe_experts",
    )(c0, nc, xs, cs, wg, wu, wd)


def _moe_unsort(ys, comb, x, mods3, base, lens, set_of_row, *, row0, n_rows, tn=1024):
    d = x.shape[1]
    n_sub = MOE_TILE // MOE_SUB
    tile0, sub0 = row0 // MOE_TILE, row0 // MOE_SUB
    return pl.pallas_call(
        functools.partial(_moe_unsort_kernel, tile0=tile0),
        out_shape=jax.ShapeDtypeStruct((n_rows, d), F32),
        grid_spec=pltpu.PrefetchScalarGridSpec(
            num_scalar_prefetch=2,
            grid=(n_rows // MOE_TILE, d // tn, n_sub),
            in_specs=[pl.BlockSpec((MOE_ROWS, tn), lambda i, j, s, b, n: (tile0 + i, j)),
                      pl.BlockSpec((MOE_SUB, LANES), lambda i, j, s, b, n: (sub0 + i * n_sub + s, 0)),
                      pl.BlockSpec((MOE_SUB, tn), lambda i, j, s, b, n: (sub0 + i * n_sub + s, j)),
                      pl.BlockSpec((1, 1, tn),
                                   lambda i, j, s, b, n: (set_of_row(row0 + i * MOE_TILE), 0,
                                                          5 * (d // tn) + j))],
            out_specs=pl.BlockSpec((MOE_SUB, tn), lambda i, j, s, b, n: (i * n_sub + s, j))),
        compiler_params=_cparams(("parallel", "parallel", "arbitrary")),
        name="moe_unsort",
    )(base, lens, ys, comb, x, mods3)


def _moe_plan(cnt):
    n_sub = MOE_TILE // MOE_SUB
    c = cnt[:, 0, N_EXPERTS:N_EXPERTS + N_GROUPS].astype(jnp.int32).reshape(-1, n_sub, N_GROUPS)
    seg = (jnp.sum(c, axis=1) + MOE_CHUNK - 1) // MOE_CHUNK * MOE_CHUNK
    seg_start = jnp.cumsum(seg, axis=1) - seg
    base = seg_start[:, None, :] + jnp.cumsum(c, axis=1) - c
    return (base.reshape(-1), c.reshape(-1), (seg_start // MOE_CHUNK).reshape(-1),
            (seg // MOE_CHUNK).reshape(-1))


def _rope_tables(t):
    n_freq = SUB_W // 4
    tok = jnp.arange(t)
    row = (tok // GRID_W).astype(F32)
    col = (tok % GRID_W).astype(F32)
    inv = ROPE_BASE ** (-jnp.arange(n_freq, dtype=F32) / n_freq)
    ar, ac = row[:, None] * inv, col[:, None] * inv
    cos = jnp.concatenate([jnp.cos(ar)] * 2 + [jnp.cos(ac)] * 2, axis=1)
    sin = jnp.concatenate([-jnp.sin(ar), jnp.sin(ar), -jnp.sin(ac), jnp.sin(ac)], axis=1)
    return jnp.tile(cos, (1, 2)), jnp.tile(sin, (1, 2))


def kernel(x_prompt, x_sample, cache_diff_k, cache_diff_v, state_ret, c, c_ctx, w_mod, b_mod,
           g_norm1, g_norm2, w_in, g_qnorm, g_knorm, lambda_qk, g_subln, ret_decay_logit, conv_w,
           conv_b, conv_ln_g, conv_ln_b, w_branch, w_out, w_router_group, b_router_group,
           w_router_expert, b_router_expert, w_exp_gate, w_exp_up, w_exp_down):
    bc, tc, d = x_prompt.shape
    bl, tl, _ = x_sample.shape
    depth = w_mod.shape[0]
    past = cache_diff_k.shape[2]
    n_ctx, n_lat = bc * tc, bl * tl
    assert bl + 1 <= 8 and n_ctx % tl == 0 and n_ctx % MOE_TILE == 0 and tl % MOE_TILE == 0

    def set_of_row(r):
        return jnp.where(r < n_ctx, 0, 1 + (r - n_ctx) // tl)

    tm_big = math.gcd(1024, math.gcd(n_ctx, tl))
    tm_small = math.gcd(512, tm_big)

    x = jnp.concatenate([x_prompt.reshape(n_ctx, d), x_sample.reshape(n_lat, d)], axis=0)
    cond8 = jnp.zeros((8, d), F32).at[0].set(c_ctx).at[1:1 + bl].set(c)
    mods = _mods(cond8, w_mod, b_mod)
    cos, sin = _rope_tables(tl)
    ck = cache_diff_k.reshape(bl, depth, past, N_HEADS * HEAD_W)
    cv = cache_diff_v.reshape(bl, depth, past, N_HEADS * HEAD_W)
    kv_shape = (bc, depth, tc, N_HEADS * HEAD_W)
    st_shape = (bc, depth, 2, N_HEADS, CHUNK, CHUNK)

    w_in_bf, w_br_bf, w_out_bf = w_in.astype(BF16), w_branch.astype(BF16), w_out.astype(BF16)
    wg_bf, wu_bf, wd_bf = w_exp_gate.astype(BF16), w_exp_up.astype(BF16), w_exp_down.astype(BF16)

    new_kv, new_st = None, None
    for l in range(depth):
        lam_init = 0.8 - 0.6 * math.exp(-0.3 * l)
        mods3 = mods[l].reshape(8, 1, 6 * d)
        gq = jnp.tile(g_qnorm[l], 2)[None]
        gk = jnp.tile(g_knorm[l], 2)[None]
        gs = g_subln[l][None]
        decay16 = ret_decay_logit[l].reshape(2 * N_HEADS, 1, 1)

        z = _inproj(x, g_norm1[l][None], mods3, w_in_bf, l, set_of_row, tm=tm_big)

        oa, ak, av = _attention(z, lambda_qk[l], gq, gk, gs, layer=l, row0=0, n_seq=bc, t=tc,
                                tq=tc, lam_init=lam_init, kv_prev=new_kv, kv_shape=kv_shape)
        new_kv = (ak, av)
        (oa,) = _attention(z, lambda_qk[l], gq, gk, gs, layer=l, row0=n_ctx, n_seq=bl, t=tl,
                           tq=math.gcd(256, tl), lam_init=lam_init, cache=(ck, cv),
                           rope_tabs=(cos, sin), oa_prev=oa)
        orr, new_st = _retention(z, decay16, layer=l, row0=0, n_seq=bc, t=tc, st_prev=new_st,
                                 st_shape=st_shape)
        (orr,) = _retention(z, decay16, layer=l, row0=n_ctx, n_seq=bl, t=tl, s0=state_ret,
                            y_prev=orr)
        cargs = (conv_w[l], conv_b[l][None], conv_ln_g[l][None], conv_ln_b[l][None])
        oc = _conv(z, *cargs, row0=0, n_seq=bc, t=tc, tt=math.gcd(512, tc))
        oc = _conv(z, *cargs, row0=n_ctx, n_seq=bl, t=tl, tt=math.gcd(512, tl), y_prev=oc)

        merged = _merge(oa, orr, oc, z, w_br_bf, l, tm=tm_small)

        w_router = jnp.zeros((d, LANES), F32).at[:, :N_EXPERTS].set(w_router_expert[l])
        w_router = w_router.at[:, N_EXPERTS:N_EXPERTS + N_GROUPS].set(w_router_group[l])
        b_router = jnp.zeros((1, LANES), F32).at[0, :N_EXPERTS].set(b_router_expert[l])
        b_router = b_router.at[0, N_EXPERTS:N_EXPERTS + N_GROUPS].set(b_router_group[l])
        x, h2, comb, cnt = _outproj(merged, x, w_out_bf, l, mods3, g_norm2[l][None],
                                    jnp.stack(_split_bf16(w_router, 2)), b_router, set_of_row,
                                    tm=tm_small)
        base, lens, c0, nc = _moe_plan(cnt)
        xs, cs = _moe_sort(h2, comb, base, lens)
        ys = _moe_experts(xs, cs, c0, nc, wg_bf, wu_bf, wd_bf, l)
        unsort = functools.partial(_moe_unsort, ys, comb, x, mods3, base, lens, set_of_row)
        if l + 1 < depth:
            x = unsort(row0=0, n_rows=n_ctx + n_lat)
        else:
            yp = unsort(row0=0, n_rows=n_ctx)
            ys_out = unsort(row0=n_ctx, n_rows=n_lat)

    return (yp.reshape(bc, tc, d), ys_out.reshape(bl, tl, d),
            new_kv[0].reshape(bc, depth, tc, N_HEADS, 2, SUB_W),
            new_kv[1].reshape(bc, depth, tc, N_HEADS, HEAD_W), new_st)
```

```python
import functools
import math

import jax
import jax.numpy as jnp
from jax import lax
from jax.experimental import pallas as pl
from jax.experimental.pallas import tpu as pltpu

F32 = jnp.float32
BF16 = jnp.bfloat16

EPS = 1e-6
LOG2E = 1.4426950408889634
GRID_W = 64
ROPE_BASE = 10000.0
HEAD_W = 128
SUB_W = 64
N_HEADS = 8
ATTN_KEY_BLOCK = 512
CTX_HEADS_PER_STEP = 4
RET_HEADS_PER_STEP = 2
CHUNK = 128
CONV_K = 31
CONV_HALO = 16
N_EXPERTS = 16
EXP_PER_GROUP = 4
N_GROUPS = 4
LANES = 128
NEG = -1e30
VMEM_LIMIT = 58 * 1024 * 1024

MOE_SUB = 256
MOE_TILE = 2048
MOE_CHUNK = 128
MOE_ROWS = MOE_TILE + N_GROUPS * MOE_CHUNK
SORT_WIN = MOE_SUB + 16
SORT_WIN_SMALL = MOE_SUB // 2
UNSORT_WIN = MOE_SUB + LANES
UNSORT_WIN_SMALL = MOE_SUB


def _cparams(sem):
    return pltpu.CompilerParams(dimension_semantics=sem, vmem_limit_bytes=VMEM_LIMIT)


def _lane(shape):
    return lax.broadcasted_iota(jnp.int32, shape, len(shape) - 1)


def _sigmoid(x):
    return 1.0 / (1.0 + jnp.exp(-x))


def _silu(x):
    return x * _sigmoid(x)


def _dot(a, b):
    return jnp.dot(a, b, preferred_element_type=F32)


def _dot_nt(a, b):
    return lax.dot_general(a, b, (((1,), (1,)), ((), ())), preferred_element_type=F32)


def _split_bf16(x, parts):
    out = []
    for _ in range(parts - 1):
        hi = x.astype(BF16)
        out.append(hi)
        x = x - hi.astype(F32)
    out.append(x.astype(BF16))
    return out


_ANY = pl.BlockSpec(memory_space=pl.ANY)
_ONCE = pl.Buffered(1)


def _mods_kernel(c_ref, w_ref, b_ref, o_ref):
    s = _silu(c_ref[...])
    o_ref[0] = jnp.dot(s, w_ref[0], preferred_element_type=F32,
                       precision=lax.Precision.HIGHEST) + b_ref[0]


def _mods(cond8, w_mod, b_mod):
    depth, d, n = w_mod.shape
    tn = 1024
    return pl.pallas_call(
        _mods_kernel,
        out_shape=jax.ShapeDtypeStruct((depth, 8, n), F32),
        grid=(depth, n // tn),
        in_specs=[pl.BlockSpec((8, d), lambda l, j: (0, 0)),
                  pl.BlockSpec((1, d, tn), lambda l, j: (l, 0, j)),
                  pl.BlockSpec((1, 1, tn), lambda l, j: (l, 0, j))],
        out_specs=pl.BlockSpec((1, 8, tn), lambda l, j: (l, 0, j)),
        compiler_params=_cparams(("parallel", "parallel")),
        name="ada_mods",
    )(cond8, w_mod, b_mod.reshape(depth, 1, n))


def _inproj_kernel(x_ref, g_ref, sh_ref, sc_ref, w_ref, o_ref, h_scr):
    @pl.when(pl.program_id(1) == 0)
    def _():
        x = x_ref[...]
        r = lax.rsqrt(jnp.mean(x * x, axis=-1, keepdims=True) + EPS)
        h = x * r * g_ref[...]
        h_scr[...] = (h * (1.0 + sc_ref[0]) + sh_ref[0]).astype(BF16)

    o_ref[...] = _dot(h_scr[...], w_ref[...]).astype(o_ref.dtype)


def _inproj(x, g, mods3, w_bf, layer, set_of_row, tm=1024, tn=1024):
    nt, d = x.shape
    n = w_bf.shape[2]
    return pl.pallas_call(
        _inproj_kernel,
        out_shape=jax.ShapeDtypeStruct((nt, n), BF16),
        grid=(nt // tm, n // tn),
        in_specs=[pl.BlockSpec((tm, d), lambda i, j: (i, 0)),
                  pl.BlockSpec((1, d), lambda i, j: (0, 0)),
                  pl.BlockSpec((1, 1, d), lambda i, j: (set_of_row(i * tm), 0, 0)),
                  pl.BlockSpec((1, 1, d), lambda i, j: (set_of_row(i * tm), 0, 1)),
                  pl.BlockSpec((None, d, tn), lambda i, j: (layer, 0, j))],
        out_specs=pl.BlockSpec((tm, tn), lambda i, j: (i, j)),
        scratch_shapes=[pltpu.VMEM((tm, d), BF16)],
        compiler_params=_cparams(("parallel", "arbitrary")),
        name="inproj",
    )(x, g, mods3, mods3, w_bf)


def _qk_norm(x, g):
    lo = _lane(x.shape) < SUB_W
    x2 = x * x
    s_lo = jnp.sum(jnp.where(lo, x2, 0.0), axis=-1, keepdims=True)
    s_hi = jnp.sum(jnp.where(lo, 0.0, x2), axis=-1, keepdims=True)
    r = jnp.where(lo, lax.rsqrt(s_lo / SUB_W + EPS), lax.rsqrt(s_hi / SUB_W + EPS))
    return x * r * g


def _rope(x, cos, sin_signed):
    first = (_lane(x.shape) % 32) < 16
    partner = jnp.where(first, pltpu.roll(x, LANES - 16, 1), pltpu.roll(x, 16, 1))
    return x * cos + partner * sin_signed


def _attn_kernel(*refs, hps, n_cache, emit_kv, n_alias, **kw):
    if hps == 1:
        return _attn_head_kernel(*refs, n_cache=n_cache, emit_kv=emit_kv, n_alias=n_alias, **kw)
    assert not n_cache
    refs = list(refs)
    n_out = 3 if emit_kv else 1
    first_out = len(refs) - 3 - n_out
    for hh in range(hps):
        hs = slice(hh * HEAD_W, (hh + 1) * HEAD_W)
        head = list(refs)
        for j in (4, 5, 6, *range(first_out, first_out + n_out)):
            head[j] = refs[j].at[:, hs]
        for j in (-3, -2, -1):
            head[j] = refs[j].at[hh]
        _attn_head_kernel(*head, n_cache=n_cache, emit_kv=emit_kv, n_alias=n_alias, **kw)


def _attn_head_kernel(*refs, tq, n_cache, rope, emit_kv, n_alias, lam_init):
    it = iter(refs)
    lam_ref, gq_ref, gk_ref, gs_ref = next(it), next(it), next(it), next(it)
    q_ref, k_ref, v_ref = next(it), next(it), next(it)
    if n_cache:
        ck_ref, cv_ref = next(it), next(it)
    if rope:
        cos_ref, sin_ref = next(it), next(it)
    for _ in range(n_alias):
        next(it)
    o_ref = next(it)
    if emit_kv:
        ak_ref, av_ref = next(it), next(it)
    kn_scr, v_scr, s_scr = next(it), next(it), next(it)

    qi = pl.program_id(2)

    @pl.when(qi == 0)
    def _():
        kn = _qk_norm(k_ref[...].astype(F32), gk_ref[...])
        if emit_kv:
            ak_ref[...] = kn
            av_ref[...] = v_ref[...].astype(F32)
        if rope:
            kn = _rope(kn, cos_ref[...], sin_ref[...])
        if n_cache:
            kn_scr[:n_cache, :] = ck_ref[...].astype(BF16)
            v_scr[:n_cache, :HEAD_W] = cv_ref[...].astype(BF16)
        kn_scr[n_cache:, :] = kn.astype(BF16)
        v_scr[n_cache:, :HEAD_W] = v_ref[...]
        v_scr[:, HEAD_W:] = jnp.ones((v_scr.shape[0], HEAD_W), BF16)

    lp = lam_ref[...]
    lam = (jnp.exp(jnp.sum(lp[0:1] * lp[1:2], axis=-1, keepdims=True))
           - jnp.exp(jnp.sum(lp[2:3] * lp[3:4], axis=-1, keepdims=True)) + lam_init)

    qn = _qk_norm(q_ref[...].astype(F32), gq_ref[...])
    if rope:
        r0 = pl.multiple_of(qi * tq, tq)
        qn = _rope(qn, cos_ref[pl.ds(r0, tq), :], sin_ref[pl.ds(r0, tq), :])
    qn = qn * (SUB_W ** -0.5 * LOG2E)
    lo = _lane(qn.shape) < SUB_W
    qs = (jnp.where(lo, qn, 0.0).astype(BF16), jnp.where(lo, 0.0, qn).astype(BF16))
    tk = kn_scr.shape[0]
    kb = math.gcd(tk, ATTN_KEY_BLOCK)
    nkb = tk // kb

    mx = [None, None]
    for j in range(nkb):
        kj = kn_scr[j * kb:(j + 1) * kb, :]
        for m in range(2):
            s = _dot_nt(qs[m], kj)
            s_scr[m, :, j * kb:(j + 1) * kb] = s
            for i in range(kb // LANES):
                blk = s[:, i * LANES:(i + 1) * LANES]
                mx[m] = blk if mx[m] is None else jnp.maximum(mx[m], blk)
    m_row = [jnp.max(mx[m], axis=-1, keepdims=True) for m in range(2)]

    acc = [jnp.zeros((tq, 2 * HEAD_W), F32), jnp.zeros((tq, 2 * HEAD_W), F32)]
    for j in range(nkb):
        vj = v_scr[j * kb:(j + 1) * kb, :]
        for m in range(2):
            p = jnp.exp2(s_scr[m, :, j * kb:(j + 1) * kb] - m_row[m])
            acc[m] = acc[m] + _dot(p.astype(BF16), vj)
    o = (acc[0][:, :HEAD_W] / acc[0][:, HEAD_W:]
         - lam * (acc[1][:, :HEAD_W] / acc[1][:, HEAD_W:]))
    o = o * lax.rsqrt(jnp.mean(o * o, axis=-1, keepdims=True) + EPS) * (gs_ref[...] * (1.0 - lam_init))
    o_ref[...] = o.astype(o_ref.dtype)


def _attention(z, lam_p, gq, gk, gs, *, layer, row0, n_seq, t, tq, lam_init, hps=1, cache=None,
               rope_tabs=None, oa_prev=None, kv_prev=None, kv_shape=None):
    n_cache = 0 if cache is None else cache[0].shape[2]
    emit_kv = kv_shape is not None
    nq = t // tq
    rb = row0 // t
    rq = row0 // tq
    hw = hps * HEAD_W
    kcol, vcol = N_HEADS // hps, 2 * N_HEADS // hps
    in_specs = [pl.BlockSpec((4, SUB_W), lambda b, h, i: (0, 0)),
                pl.BlockSpec((1, HEAD_W), lambda b, h, i: (0, 0)),
                pl.BlockSpec((1, HEAD_W), lambda b, h, i: (0, 0)),
                pl.BlockSpec((1, HEAD_W), lambda b, h, i: (0, 0)),
                pl.BlockSpec((tq, hw), lambda b, h, i: (rq + b * nq + i, h)),
                pl.BlockSpec((t, hw), lambda b, h, i: (rb + b, kcol + h)),
                pl.BlockSpec((t, hw), lambda b, h, i: (rb + b, vcol + h))]
    args = [lam_p, gq, gk, gs, z, z, z]
    if n_cache:
        cspec = pl.BlockSpec((None, None, n_cache, HEAD_W), lambda b, h, i: (b, layer, 0, h))
        in_specs += [cspec, cspec]
        args += [cache[0], cache[1]]
    if rope_tabs is not None:
        in_specs += [pl.BlockSpec((t, HEAD_W), lambda b, h, i: (0, 0)),
                     pl.BlockSpec((t, HEAD_W), lambda b, h, i: (0, 0))]
        args += list(rope_tabs)
    aliases = {}
    n_alias = 0
    if oa_prev is not None:
        aliases[len(args)] = 0
        in_specs.append(_ANY)
        args.append(oa_prev)
        n_alias += 1
    if kv_prev is not None:
        for j, a in enumerate(kv_prev):
            aliases[len(args)] = 1 + j
            in_specs.append(_ANY)
            args.append(a)
            n_alias += 1
    out_shape = [jax.ShapeDtypeStruct((z.shape[0], N_HEADS * HEAD_W), BF16)]
    out_specs = [pl.BlockSpec((tq, hw), lambda b, h, i: (rq + b * nq + i, h))]
    if emit_kv:
        out_shape += [jax.ShapeDtypeStruct(kv_shape, F32)] * 2
        out_specs += [pl.BlockSpec((None, None, t, hw), lambda b, h, i: (b, layer, 0, h))] * 2
    per_head = () if hps == 1 else (hps,)
    return pl.pallas_call(
        functools.partial(_attn_kernel, hps=hps, tq=tq, n_cache=n_cache,
                          rope=rope_tabs is not None, emit_kv=emit_kv, n_alias=n_alias,
                          lam_init=lam_init),
        out_shape=out_shape,
        grid=(n_seq, N_HEADS // hps, nq),
        in_specs=in_specs,
        out_specs=out_specs,
        scratch_shapes=[pltpu.VMEM(per_head + (n_cache + t, HEAD_W), BF16),
                        pltpu.VMEM(per_head + (n_cache + t, 2 * HEAD_W), BF16),
                        pltpu.VMEM(per_head + (2, tq, n_cache + t), F32)],
        input_output_aliases=aliases,
        compiler_params=_cparams(("parallel", "parallel", "arbitrary")),
        name="diff_attn_lat" if n_cache else "diff_attn_ctx",
    )(*args)


def _log_sigmoid(x):
    return jnp.minimum(x, 0.0) - jnp.log1p(jnp.exp(-jnp.abs(x)))


def _group_norm(o):
    mu = jnp.mean(o, axis=-1, keepdims=True)
    d = o - mu
    return d * lax.rsqrt(jnp.mean(d * d, axis=-1, keepdims=True) + EPS)


def _ret_kernel(*refs, n_chunks, has_s0, emit_state, n_alias):
    it = iter(refs)
    dlf_ref, dlb_ref = next(it), next(it)
    q_ref, k_ref, v_ref, gf_ref, gb_ref = (next(it) for _ in range(5))
    if has_s0:
        s0f_ref, s0b_ref = next(it), next(it)
    for _ in range(n_alias):
        next(it)
    y_ref = next(it)
    if emit_state:
        st_ref = next(it)
    of_scr, ob_scr, s_scr, tab_scr = next(it), next(it), next(it), next(it)

    c = CHUNK
    nh = RET_HEADS_PER_STEP
    row = lax.broadcasted_iota(jnp.int32, (c, c), 0).astype(F32)
    col = lax.broadcasted_iota(jnp.int32, (c, c), 1).astype(F32)
    diff = row - col
    k_scale = HEAD_W ** -0.5
    cdec = []
    for hh in range(nh):
        lgf = _log_sigmoid(dlf_ref[hh])
        lgb = _log_sigmoid(dlb_ref[hh])
        tab_scr[hh, 0, 0] = jnp.where(diff >= 0, jnp.exp(jnp.maximum(diff, 0.0) * lgf), 0.0) * k_scale
        tab_scr[hh, 0, 1] = jnp.exp((row + 1.0) * lgf)
        tab_scr[hh, 0, 2] = jnp.exp((c - 1.0 - row) * lgf) * k_scale
        tab_scr[hh, 1, 0] = jnp.where(diff <= 0, jnp.exp(jnp.maximum(-diff, 0.0) * lgb), 0.0) * k_scale
        tab_scr[hh, 1, 1] = jnp.exp((c - row) * lgb)
        tab_scr[hh, 1, 2] = jnp.exp(row * lgb) * k_scale
        cdec.append((jnp.exp(c * lgf), jnp.exp(c * lgb)))
        for dr in range(2):
            if has_s0:
                s_scr[hh, dr] = (s0f_ref, s0b_ref)[dr][hh]
            else:
                s_scr[hh, dr] = jnp.zeros((c, c), F32)

    chains = [(hh, dr) for hh in range(nh) for dr in range(2)]

    def step(i, carry):
        rows = [pl.multiple_of((i if dr == 0 else n_chunks - 1 - i) * c, c) for _, dr in chains]
        cols = [slice(hh * HEAD_W, (hh + 1) * HEAD_W) for hh, _ in chains]
        qc = [q_ref[pl.ds(r, c), hs] for r, hs in zip(rows, cols)]
        kc = [k_ref[pl.ds(r, c), hs] for r, hs in zip(rows, cols)]
        vc = [v_ref[pl.ds(r, c), hs] for r, hs in zip(rows, cols)]
        st = [s_scr[hh, dr] for hh, dr in chains]
        att = [(_dot_nt(q, k) * tab_scr[hh, dr, 0]).astype(BF16)
               for q, k, (hh, dr) in zip(qc, kc, chains)]
        cross = [_dot(q, s.astype(BF16)) * tab_scr[hh, dr, 1] for q, s, (hh, dr) in zip(qc, st, chains)]
        kd = [(k.astype(F32) * tab_scr[hh, dr, 2]).T.astype(BF16) for k, (hh, dr) in zip(kc, chains)]
        for n, (hh, dr) in enumerate(chains):
            (of_scr, ob_scr)[dr][pl.ds(rows[n], c), cols[n]] = _dot(att[n], vc[n]) + cross[n]
        for n, (hh, dr) in enumerate(chains):
            s_scr[hh, dr] = st[n] * cdec[hh][dr] + _dot(kd[n], vc[n])
        return carry

    lax.fori_loop(0, n_chunks, step, 0)

    def combine(ci, carry):
        r = pl.multiple_of(ci * c, c)
        for hh in range(nh):
            hs = slice(hh * HEAD_W, (hh + 1) * HEAD_W)
            y = (_group_norm(of_scr[pl.ds(r, c), hs]) * _silu(gf_ref[pl.ds(r, c), hs].astype(F32))
                 + _group_norm(ob_scr[pl.ds(r, c), hs]) * _silu(gb_ref[pl.ds(r, c), hs].astype(F32)))
            y_ref[pl.ds(r, c), hs] = y.astype(y_ref.dtype)
        return carry

    lax.fori_loop(0, n_chunks, combine, 0)
    if emit_state:
        for hh in range(nh):
            for dr in range(2):
                st_ref[dr, hh] = s_scr[hh, dr]


def _retention(z, decay16, *, layer, row0, n_seq, t, s0=None, y_prev=None, st_prev=None,
               st_shape=None):
    nh = RET_HEADS_PER_STEP
    hw = nh * HEAD_W
    rb = row0 // t
    hp = N_HEADS // nh
    emit_state = st_shape is not None
    col = lambda j: (lambda b, h: (rb + b, j * hp + h))
    in_specs = [pl.BlockSpec((nh, 1, 1), lambda b, h: (h, 0, 0)),
                pl.BlockSpec((nh, 1, 1), lambda b, h: (hp + h, 0, 0))]
    in_specs += [pl.BlockSpec((t, hw), col(j)) for j in (3, 4, 5, 6, 7)]
    args = [decay16, decay16, z, z, z, z, z]
    if s0 is not None:
        for dr in range(2):
            in_specs.append(pl.BlockSpec((None, None, None, nh, CHUNK, CHUNK),
                                         lambda b, h, dr=dr: (b, layer, dr, h, 0, 0)))
        args += [s0, s0]
    aliases = {}
    n_alias = 0
    if y_prev is not None:
        aliases[len(args)] = 0
        in_specs.append(_ANY)
        args.append(y_prev)
        n_alias += 1
    if st_prev is not None:
        aliases[len(args)] = 1
        in_specs.append(_ANY)
        args.append(st_prev)
        n_alias += 1
    out_shape = [jax.ShapeDtypeStruct((z.shape[0], N_HEADS * HEAD_W), BF16)]
    out_specs = [pl.BlockSpec((t, hw), lambda b, h: (rb + b, h))]
    if emit_state:
        out_shape.append(jax.ShapeDtypeStruct(st_shape, F32))
        out_specs.append(pl.BlockSpec((None, None, 2, nh, CHUNK, CHUNK),
                                      lambda b, h: (b, layer, 0, h, 0, 0)))
    return pl.pallas_call(
        functools.partial(_ret_kernel, n_chunks=t // CHUNK, has_s0=s0 is not None,
                          emit_state=emit_state, n_alias=n_alias),
        out_shape=out_shape,
        grid=(n_seq, hp),
        in_specs=in_specs,
        out_specs=out_specs,
        scratch_shapes=[pltpu.VMEM((t, hw), F32), pltpu.VMEM((t, hw), F32),
                        pltpu.VMEM((nh, 2, CHUNK, CHUNK), F32),
                        pltpu.VMEM((nh, 2, 3, CHUNK, CHUNK), F32)],
        input_output_aliases=aliases,
        compiler_params=_cparams(("parallel", "parallel")),
        name="retention_lat" if s0 is not None else "retention_ctx",
    )(*args)


def _conv_kernel(*refs, tt, nt, rblk, n_alias):
    (a_ref, g_ref, ap_ref, gp_ref, an_ref, gn_ref, w_ref, b_ref, lg_ref, lb_ref) = refs[:10]
    o_ref, u_scr, sh_scr, y_scr = refs[10 + n_alias:]
    ti = pl.program_id(1)
    halo = CONV_HALO
    cw = a_ref.shape[1]

    def glu(a, g):
        return a[...].astype(F32) * _sigmoid(g[...].astype(F32))

    u_scr[halo:halo + tt, :] = glu(a_ref, g_ref)
    u_scr[0:halo, :] = jnp.where(ti > 0, glu(ap_ref, gp_ref), 0.0)
    u_scr[halo + tt:, :] = jnp.where(ti < nt - 1, glu(an_ref, gn_ref), 0.0)

    ext = tt + 2 * halo
    for cb in range(cw // LANES):
        cs = slice(cb * LANES, (cb + 1) * LANES)
        x = u_scr[:, cs]
        sh_scr[0] = x
        for b in range(1, 8):
            sh_scr[b] = pltpu.roll(x, ext - b, 0)

        def rows(ri, carry):
            r = pl.multiple_of(ri * rblk, rblk)
            acc = jnp.zeros((rblk, LANES), F32)
            for k in range(CONV_K):
                off = halo - CONV_K // 2 + k
                acc = acc + sh_scr[off % 8, pl.ds(r + 8 * (off // 8), rblk), :] * w_ref[k:k + 1, cs]
            y_scr[pl.ds(r, rblk), cs] = acc + b_ref[:, cs]
            return carry

        lax.fori_loop(0, tt // rblk, rows, 0)

    y = y_scr[...]
    mu = jnp.mean(y, axis=-1, keepdims=True)
    d = y - mu
    yn = d * lax.rsqrt(jnp.mean(d * d, axis=-1, keepdims=True) + EPS) * lg_ref[...] + lb_ref[...]
    o_ref[...] = _silu(yn).astype(o_ref.dtype)


def _conv(z, w, b, lg, lb, *, row0, n_seq, t, tt, y_prev=None):
    cw = w.shape[1]
    nt = t // tt
    total_h = z.shape[0] // CONV_HALO
    acol = 8192 // cw
    r_t = row0 // tt
    r_h = row0 // CONV_HALO
    per_h = tt // CONV_HALO

    def cur(j):
        return lambda s, i: (r_t + s * nt + i, acol + j)

    def prev(j):
        return lambda s, i: (jnp.maximum(r_h + (s * nt + i) * per_h - 1, 0), acol + j)

    def nxt(j):
        return lambda s, i: (jnp.minimum(r_h + (s * nt + i + 1) * per_h, total_h - 1), acol + j)

    full = lambda s, i: (0, 0)
    in_specs = [pl.BlockSpec((tt, cw), cur(0)), pl.BlockSpec((tt, cw), cur(1)),
                pl.BlockSpec((CONV_HALO, cw), prev(0)), pl.BlockSpec((CONV_HALO, cw), prev(1)),
                pl.BlockSpec((CONV_HALO, cw), nxt(0)), pl.BlockSpec((CONV_HALO, cw), nxt(1)),
                pl.BlockSpec((CONV_K, cw), full), pl.BlockSpec((1, cw), full),
                pl.BlockSpec((1, cw), full), pl.BlockSpec((1, cw), full)]
    args = [z, z, z, z, z, z, w, b, lg, lb]
    aliases = {}
    if y_prev is not None:
        aliases[len(args)] = 0
        in_specs.append(_ANY)
        args.append(y_prev)
    return pl.pallas_call(
        functools.partial(_conv_kernel, tt=tt, nt=nt, rblk=64, n_alias=len(aliases)),
        out_shape=jax.ShapeDtypeStruct((z.shape[0], cw), BF16),
        grid=(n_seq, nt),
        in_specs=in_specs,
        out_specs=pl.BlockSpec((tt, cw), lambda s, i: (r_t + s * nt + i, 0)),
        scratch_shapes=[pltpu.VMEM((tt + 2 * CONV_HALO, cw), F32),
                        pltpu.VMEM((8, tt + 2 * CONV_HALO, LANES), F32),
                        pltpu.VMEM((tt, cw), F32)],
        input_output_aliases=aliases,
        compiler_params=_cparams(("parallel", "parallel")),
        name="conformer_conv",
    )(*args)


def _merge_kernel(a_ref, r_ref, c_ref, g0_ref, g1_ref, g2_ref, w_ref, o_ref):
    acc = _sigmoid(g0_ref[...].astype(F32)) * _dot(a_ref[...], w_ref[0])
    acc += _sigmoid(g1_ref[...].astype(F32)) * _dot(r_ref[...], w_ref[1])
    acc += _sigmoid(g2_ref[...].astype(F32)) * _dot(c_ref[...], w_ref[2])
    o_ref[...] = acc.astype(o_ref.dtype)


def _merge(oa, orr, oc, z, w_br, layer, tm=512, tn=1024):
    nt, bw = oa.shape
    d = w_br.shape[3]
    g0 = 10240 // tn
    gs = d // tn
    br = pl.BlockSpec((tm, bw), lambda i, j: (i, 0))
    return pl.pallas_call(
        _merge_kernel,
        out_shape=jax.ShapeDtypeStruct((nt, d), BF16),
        grid=(nt // tm, d // tn),
        in_specs=[br, br, br,
                  pl.BlockSpec((tm, tn), lambda i, j: (i, g0 + j)),
                  pl.BlockSpec((tm, tn), lambda i, j: (i, g0 + gs + j)),
                  pl.BlockSpec((tm, tn), lambda i, j: (i, g0 + 2 * gs + j)),
                  pl.BlockSpec((None, 3, bw, tn), lambda i, j: (layer, 0, 0, j))],
        out_specs=pl.BlockSpec((tm, tn), lambda i, j: (i, j)),
        compiler_params=_cparams(("parallel", "parallel")),
        name="branch_merge",
    )(oa, orr, oc, z, z, z, w_br)


def _outproj_kernel(m_ref, x_ref, w_ref, g1_ref, gn_ref, sh_ref, sc_ref, wr_ref, br_ref,
                    xo_ref, h_ref, cb_ref, cnt_ref):
    x = x_ref[...] + g1_ref[0] * _dot(m_ref[...], w_ref[...])
    xo_ref[...] = x
    h = x * lax.rsqrt(jnp.mean(x * x, axis=-1, keepdims=True) + EPS) * gn_ref[...]
    h = h * (1.0 + sc_ref[0]) + sh_ref[0]
    h_hi = h.astype(BF16)
    h_ref[...] = h_hi
    h_lo = (h - h_hi.astype(F32)).astype(BF16)

    lg = _dot(h_hi, wr_ref[0]) + _dot(h_hi, wr_ref[1]) + _dot(h_lo, wr_ref[0]) + br_ref[...]
    lane = _lane(lg.shape).astype(F32)
    big = float(LANES)

    def first_max(v):
        m = jnp.max(v, axis=-1, keepdims=True)
        return m, jnp.min(jnp.where(v == m, lane, big), axis=-1, keepdims=True)

    is_g = (lane >= N_EXPERTS) & (lane < N_EXPERTS + N_GROUPS)
    gl = jnp.where(is_g, lg, NEG)
    gmax, gidx = first_max(gl)
    g_w = 1.0 / jnp.sum(jnp.exp(gl - gmax), axis=-1, keepdims=True)
    e_lo = (gidx - N_EXPERTS) * EXP_PER_GROUP
    is_e = (lane >= e_lo) & (lane < e_lo + EXP_PER_GROUP)
    el = jnp.where(is_e, lg, NEG)
    e1, i1 = first_max(el)
    e2, i2 = first_max(jnp.where(lane == i1, NEG, el))
    p2 = jnp.exp(e2 - e1)
    w1 = g_w / (1.0 + p2)
    onehot = jnp.where(lane == gidx, 1.0, 0.0)
    cb_ref[...] = jnp.where(lane == i1, w1, 0.0) + jnp.where(lane == i2, w1 * p2, 0.0) + onehot
    for s in range(cnt_ref.shape[0]):
        cnt_ref[s] = jnp.sum(onehot[s * MOE_SUB:(s + 1) * MOE_SUB], axis=0, keepdims=True)


def _outproj(merged, x, w_o, layer, mods3, g2, w_router, b_router, set_of_row, tm=512):
    nt, d = x.shape
    mod = lambda c: pl.BlockSpec((1, 1, d), lambda i: (set_of_row(i * tm), 0, c))
    full = lambda i: (0, 0)
    ns = tm // MOE_SUB
    return pl.pallas_call(
        _outproj_kernel,
        out_shape=[jax.ShapeDtypeStruct((nt, d), F32),
                   jax.ShapeDtypeStruct((nt, d), BF16),
                   jax.ShapeDtypeStruct((nt, LANES), F32),
                   jax.ShapeDtypeStruct((nt // MOE_SUB, 1, LANES), F32)],
        grid=(nt // tm,),
        in_specs=[pl.BlockSpec((tm, d), lambda i: (i, 0)),
                  pl.BlockSpec((tm, d), lambda i: (i, 0)),
                  pl.BlockSpec((None, d, d), lambda i: (layer, 0, 0)),
                  mod(2), pl.BlockSpec((1, d), full), mod(3), mod(4),
                  pl.BlockSpec((2, d, LANES), lambda i: (0, 0, 0)), pl.BlockSpec((1, LANES), full)],
        out_specs=[pl.BlockSpec((tm, d), lambda i: (i, 0)),
                   pl.BlockSpec((tm, d), lambda i: (i, 0)),
                   pl.BlockSpec((tm, LANES), lambda i: (i, 0)),
                   pl.BlockSpec((ns, 1, LANES), lambda i: (i, 0, 0))],
        compiler_params=_cparams(("parallel",)),
        name="outproj_router",
    )(merged, x, w_o, mods3, g2, mods3, mods3, w_router, b_router)


def _group_onehot(cb):
    lane = _lane(cb.shape)
    return jnp.where((lane >= N_EXPERTS) & (lane < N_EXPERTS + N_GROUPS), cb, 0.0)


def _window_start(base, win):
    w0 = jnp.minimum((base // 16) * 16, MOE_ROWS - win)
    return pl.multiple_of(w0, 16)


def _moe_sort_kernel(base_ref, len_ref, h_ref, cb_ref, xs_ref, cs_ref, acc_scr):
    i, s = pl.program_id(0), pl.program_id(1)
    n_sub = pl.num_programs(1)

    @pl.when(s == 0)
    def _():
        acc_scr[...] = jnp.zeros_like(acc_scr)
        cs_ref[...] = jnp.zeros_like(cs_ref)

    cb = cb_ref[...]
    g1h = _group_onehot(cb).astype(BF16)
    eye = (lax.broadcasted_iota(jnp.int32, (LANES, LANES), 0)
           == lax.broadcasted_iota(jnp.int32, (LANES, LANES), 1)).astype(BF16)
    gt = _dot_nt(eye, g1h)
    before = (lax.broadcasted_iota(jnp.int32, (MOE_SUB, MOE_SUB), 0)
              < lax.broadcasted_iota(jnp.int32, (MOE_SUB, MOE_SUB), 1)).astype(BF16)
    rank_t = _dot(gt.astype(BF16), before)
    sub = lax.broadcasted_iota(jnp.int32, (LANES, 1), 0)
    bases = [base_ref[(i * n_sub + s) * N_GROUPS + g] for g in range(N_GROUPS)]
    basec = jnp.zeros((LANES, 1), F32)
    for g in range(N_GROUPS):
        basec = jnp.where(sub == N_EXPERTS + g, bases[g].astype(F32), basec)
    dest = jnp.sum(gt * (basec + rank_t), axis=0, keepdims=True)
    cparts = _split_bf16(cb, 3)
    lens = [len_ref[(i * n_sub + s) * N_GROUPS + g] for g in range(N_GROUPS)]
    longest = functools.reduce(jnp.maximum, lens)

    def place(win):
        w0 = [_window_start(bases[g], win) for g in range(N_GROUPS)]
        perm = []
        for g in range(N_GROUPS):
            slot = lax.broadcasted_iota(jnp.int32, (win, MOE_SUB), 0).astype(F32) + w0[g].astype(F32)
            ing = gt[N_EXPERTS + g:N_EXPERTS + g + 1, :]
            perm.append(jnp.where((slot == dest) & (ing > 0.5), 1.0, 0.0).astype(BF16))
        perm = jnp.concatenate(perm, axis=0)
        xg = _dot(perm, h_ref[...])
        cg = _dot(perm, cparts[0]) + _dot(perm, cparts[1]) + _dot(perm, cparts[2])
        for g in range(N_GROUPS):
            acc_scr[pl.ds(w0[g], win), :] += xg[g * win:(g + 1) * win]
            cs_ref[pl.ds(w0[g], win), :] += cg[g * win:(g + 1) * win]

    @pl.when(longest <= SORT_WIN_SMALL - 16)
    def _():
        place(SORT_WIN_SMALL)

    @pl.when(longest > SORT_WIN_SMALL - 16)
    def _():
        place(SORT_WIN)

    @pl.when(s == n_sub - 1)
    def _():
        xs_ref[...] = acc_scr[...].astype(BF16)


def _moe_expert_kernel(c0_ref, nc_ref, xs_ref, cs_ref, wg_ref, wu_ref, wd_ref, ys_ref):
    i, e = pl.program_id(0), pl.program_id(1)
    g = e // EXP_PER_GROUP

    @pl.when(e == 0)
    def _():
        ys_ref[...] = jnp.zeros_like(ys_ref)

    c0 = c0_ref[i * N_GROUPS + g]
    nc = nc_ref[i * N_GROUPS + g]

    def run(ci, rows):
        r = pl.multiple_of(ci * MOE_CHUNK, MOE_CHUNK)
        x = xs_ref[pl.ds(r, rows), :]
        cw = cs_ref[pl.ds(r, rows), :]
        w_e = jnp.sum(jnp.where(_lane(cw.shape) == e, cw, 0.0), axis=-1, keepdims=True)
        act = _silu(_dot(x, wg_ref[0])) * _dot(x, wu_ref[0]) * w_e
        ys_ref[pl.ds(r, rows), :] += _dot(act.astype(BF16), wd_ref[0])

    def quad(k, carry):
        run(c0 + 4 * k, 4 * MOE_CHUNK)
        return carry

    lax.fori_loop(0, nc // 4, quad, 0)

    @pl.when(nc % 4 >= 2)
    def _():
        run(c0 + nc // 4 * 4, 2 * MOE_CHUNK)

    @pl.when(nc % 2 == 1)
    def _():
        run(c0 + nc - 1, MOE_CHUNK)


def _moe_unsort_kernel(base_ref, len_ref, ys_ref, cb_ref, x_ref, g2_ref, o_ref, *, tile0):
    i, s = pl.program_id(0), pl.program_id(2)
    n_sub = pl.num_programs(2)
    cb = cb_ref[...]
    g1h = _group_onehot(cb)
    after = (lax.broadcasted_iota(jnp.int32, (MOE_SUB, MOE_SUB), 1)
             < lax.broadcasted_iota(jnp.int32, (MOE_SUB, MOE_SUB), 0)).astype(BF16)
    rank = _dot(after, g1h.astype(BF16))
    lane = _lane((1, LANES))
    bases = [base_ref[((tile0 + i) * n_sub + s) * N_GROUPS + g] for g in range(N_GROUPS)]
    basev = jnp.zeros((1, LANES), F32)
    for g in range(N_GROUPS):
        basev = jnp.where(lane == N_EXPERTS + g, bases[g].astype(F32), basev)
    dest = jnp.sum(g1h * (basev + rank), axis=-1, keepdims=True)
    lens = [len_ref[((tile0 + i) * n_sub + s) * N_GROUPS + g] for g in range(N_GROUPS)]
    longest = functools.reduce(jnp.maximum, lens)

    def gather(win):
        acc = jnp.zeros(o_ref.shape, F32)
        for g in range(N_GROUPS):
            w0 = _window_start(bases[g], win)
            slot = lax.broadcasted_iota(jnp.int32, (MOE_SUB, win), 1).astype(F32) + w0.astype(F32)
            ing = jnp.sum(jnp.where(_lane(cb.shape) == N_EXPERTS + g, cb, 0.0), axis=-1, keepdims=True)
            perm_t = jnp.where((slot == dest) & (ing > 0.5), 1.0, 0.0).astype(BF16)
            acc += _dot(perm_t, ys_ref[pl.ds(w0, win), :].astype(BF16))
        o_ref[...] = x_ref[...] + g2_ref[0] * acc

    @pl.when(longest <= UNSORT_WIN_SMALL - 16)
    def _():
        gather(UNSORT_WIN_SMALL)

    @pl.when(longest > UNSORT_WIN_SMALL - 16)
    def _():
        gather(UNSORT_WIN)


def _moe_sort(h2, comb, base, lens):
    nt, d = h2.shape
    n_tiles, n_sub = nt // MOE_TILE, MOE_TILE // MOE_SUB
    return pl.pallas_call(
        _moe_sort_kernel,
        out_shape=[jax.ShapeDtypeStruct((n_tiles * MOE_ROWS, d), BF16),
                   jax.ShapeDtypeStruct((n_tiles * MOE_ROWS, LANES), F32)],
        grid_spec=pltpu.PrefetchScalarGridSpec(
            num_scalar_prefetch=2,
            grid=(n_tiles, n_sub),
            in_specs=[pl.BlockSpec((MOE_SUB, d), lambda i, s, b, n: (i * n_sub + s, 0)),
                      pl.BlockSpec((MOE_SUB, LANES), lambda i, s, b, n: (i * n_sub + s, 0))],
            out_specs=[pl.BlockSpec((MOE_ROWS, d), lambda i, s, b, n: (i, 0), pipeline_mode=_ONCE),
                       pl.BlockSpec((MOE_ROWS, LANES), lambda i, s, b, n: (i, 0), pipeline_mode=_ONCE)],
            scratch_shapes=[pltpu.VMEM((MOE_ROWS, d), F32)]),
        compiler_params=_cparams(("parallel", "arbitrary")),
        name="moe_sort",
    )(base, lens, h2, comb)


def _moe_experts(xs, cs, c0, nc, wg, wu, wd, layer):
    d = xs.shape[1]
    n_tiles = xs.shape[0] // MOE_ROWS
    _, ne, _, ff = wg.shape
    return pl.pallas_call(
        _moe_expert_kernel,
        out_shape=jax.ShapeDtypeStruct((n_tiles * MOE_ROWS, d), F32),
        grid_spec=pltpu.PrefetchScalarGridSpec(
            num_scalar_prefetch=2,
            grid=(n_tiles, ne),
            in_specs=[pl.BlockSpec((MOE_ROWS, d), lambda i, e, a, b: (i, 0), pipeline_mode=_ONCE),
                      pl.BlockSpec((MOE_ROWS, LANES), lambda i, e, a, b: (i, 0), pipeline_mode=_ONCE),
                      pl.BlockSpec((None, 1, d, ff), lambda i, e, a, b: (layer, e, 0, 0)),
                      pl.BlockSpec((None, 1, d, ff), lambda i, e, a, b: (layer, e, 0, 0)),
                      pl.BlockSpec((None, 1, ff, d), lambda i, e, a, b: (layer, e, 0, 0))],
            out_specs=pl.BlockSpec((MOE_ROWS, d), lambda i, e, a, b: (i, 0), pipeline_mode=_ONCE)),
        compiler_params=_cparams(("parallel", "arbitrary")),
        name="moe_experts",
    )(c0, nc, xs, cs, wg, wu, wd)


def _moe_unsort(ys, comb, x, mods3, base, lens, set_of_row, *, row0, n_rows, tn=1024):
    d = x.shape[1]
    n_sub = MOE_TILE // MOE_SUB
    tile0, sub0 = row0 // MOE_TILE, row0 // MOE_SUB
    return pl.pallas_call(
        functools.partial(_moe_unsort_kernel, tile0=tile0),
        out_shape=jax.ShapeDtypeStruct((n_rows, d), F32),
        grid_spec=pltpu.PrefetchScalarGridSpec(
            num_scalar_prefetch=2,
            grid=(n_rows // MOE_TILE, d // tn, n_sub),
            in_specs=[pl.BlockSpec((MOE_ROWS, tn), lambda i, j, s, b, n: (tile0 + i, j)),
                      pl.BlockSpec((MOE_SUB, LANES), lambda i, j, s, b, n: (sub0 + i * n_sub + s, 0)),
                      pl.BlockSpec((MOE_SUB, tn), lambda i, j, s, b, n: (sub0 + i * n_sub + s, j)),
                      pl.BlockSpec((1, 1, tn),
                                   lambda i, j, s, b, n: (set_of_row(row0 + i * MOE_TILE), 0,
                                                          5 * (d // tn) + j))],
            out_specs=pl.BlockSpec((MOE_SUB, tn), lambda i, j, s, b, n: (i * n_sub + s, j))),
        compiler_params=_cparams(("parallel", "parallel", "arbitrary")),
        name="moe_unsort",
    )(base, lens, ys, comb, x, mods3)


def _moe_plan(cnt):
    n_sub = MOE_TILE // MOE_SUB
    c = cnt[:, 0, N_EXPERTS:N_EXPERTS + N_GROUPS].astype(jnp.int32).reshape(-1, n_sub, N_GROUPS)
    seg = (jnp.sum(c, axis=1) + MOE_CHUNK - 1) // MOE_CHUNK * MOE_CHUNK
    seg_start = jnp.cumsum(seg, axis=1) - seg
    base = seg_start[:, None, :] + jnp.cumsum(c, axis=1) - c
    return (base.reshape(-1), c.reshape(-1), (seg_start // MOE_CHUNK).reshape(-1),
            (seg // MOE_CHUNK).reshape(-1))


def _rope_tables(t):
    n_freq = SUB_W // 4
    tok = jnp.arange(t)
    row = (tok // GRID_W).astype(F32)
    col = (tok % GRID_W).astype(F32)
    inv = ROPE_BASE ** (-jnp.arange(n_freq, dtype=F32) / n_freq)
    ar, ac = row[:, None] * inv, col[:, None] * inv
    cos = jnp.concatenate([jnp.cos(ar)] * 2 + [jnp.cos(ac)] * 2, axis=1)
    sin = jnp.concatenate([-jnp.sin(ar), jnp.sin(ar), -jnp.sin(ac), jnp.sin(ac)], axis=1)
    return jnp.tile(cos, (1, 2)), jnp.tile(sin, (1, 2))


def kernel(x_prompt, x_sample, cache_diff_k, cache_diff_v, state_ret, c, c_ctx, w_mod, b_mod,
           g_norm1, g_norm2, w_in, g_qnorm, g_knorm, lambda_qk, g_subln, ret_decay_logit, conv_w,
           conv_b, conv_ln_g, conv_ln_b, w_branch, w_out, w_router_group, b_router_group,
           w_router_expert, b_router_expert, w_exp_gate, w_exp_up, w_exp_down):
    bc, tc, d = x_prompt.shape
    bl, tl, _ = x_sample.shape
    depth = w_mod.shape[0]
    past = cache_diff_k.shape[2]
    n_ctx, n_lat = bc * tc, bl * tl
    assert bl + 1 <= 8 and n_ctx % tl == 0 and n_ctx % MOE_TILE == 0 and tl % MOE_TILE == 0

    def set_of_row(r):
        return jnp.where(r < n_ctx, 0, 1 + (r - n_ctx) // tl)

    tm_big = math.gcd(1024, math.gcd(n_ctx, tl))
    tm_small = math.gcd(512, tm_big)

    x = jnp.concatenate([x_prompt.reshape(n_ctx, d), x_sample.reshape(n_lat, d)], axis=0)
    cond8 = jnp.zeros((8, d), F32).at[0].set(c_ctx).at[1:1 + bl].set(c)
    mods = _mods(cond8, w_mod, b_mod)
    cos, sin = _rope_tables(tl)
    ck = cache_diff_k.reshape(bl, depth, past, N_HEADS * HEAD_W)
    cv = cache_diff_v.reshape(bl, depth, past, N_HEADS * HEAD_W)
    kv_shape = (bc, depth, tc, N_HEADS * HEAD_W)
    st_shape = (bc, depth, 2, N_HEADS, CHUNK, CHUNK)

    w_in_bf, w_br_bf, w_out_bf = w_in.astype(BF16), w_branch.astype(BF16), w_out.astype(BF16)
    wg_bf, wu_bf, wd_bf = w_exp_gate.astype(BF16), w_exp_up.astype(BF16), w_exp_down.astype(BF16)

    new_kv, new_st = None, None
    for l in range(depth):
        lam_init = 0.8 - 0.6 * math.exp(-0.3 * l)
        mods3 = mods[l].reshape(8, 1, 6 * d)
        gq = jnp.tile(g_qnorm[l], 2)[None]
        gk = jnp.tile(g_knorm[l], 2)[None]
        gs = g_subln[l][None]
        decay16 = ret_decay_logit[l].reshape(2 * N_HEADS, 1, 1)

        z = _inproj(x, g_norm1[l][None], mods3, w_in_bf, l, set_of_row, tm=tm_big)

        oa, ak, av = _attention(z, lambda_qk[l], gq, gk, gs, layer=l, row0=0, n_seq=bc, t=tc,
                                tq=tc, lam_init=lam_init, hps=CTX_HEADS_PER_STEP, kv_prev=new_kv,
                                kv_shape=kv_shape)
        new_kv = (ak, av)
        (oa,) = _attention(z, lambda_qk[l], gq, gk, gs, layer=l, row0=n_ctx, n_seq=bl, t=tl,
                           tq=math.gcd(512, tl), lam_init=lam_init, cache=(ck, cv),
                           rope_tabs=(cos, sin), oa_prev=oa)
        orr, new_st = _retention(z, decay16, layer=l, row0=0, n_seq=bc, t=tc, st_prev=new_st,
                                 st_shape=st_shape)
        (orr,) = _retention(z, decay16, layer=l, row0=n_ctx, n_seq=bl, t=tl, s0=state_ret,
                            y_prev=orr)
        cargs = (conv_w[l], conv_b[l][None], conv_ln_g[l][None], conv_ln_b[l][None])
        oc = _conv(z, *cargs, row0=0, n_seq=bc, t=tc, tt=math.gcd(512, tc))
        oc = _conv(z, *cargs, row0=n_ctx, n_seq=bl, t=tl, tt=math.gcd(512, tl), y_prev=oc)

        merged = _merge(oa, orr, oc, z, w_br_bf, l, tm=tm_big)

        w_router = jnp.zeros((d, LANES), F32).at[:, :N_EXPERTS].set(w_router_expert[l])
        w_router = w_router.at[:, N_EXPERTS:N_EXPERTS + N_GROUPS].set(w_router_group[l])
        b_router = jnp.zeros((1, LANES), F32).at[0, :N_EXPERTS].set(b_router_expert[l])
        b_router = b_router.at[0, N_EXPERTS:N_EXPERTS + N_GROUPS].set(b_router_group[l])
        x, h2, comb, cnt = _outproj(merged, x, w_out_bf, l, mods3, g_norm2[l][None],
                                    jnp.stack(_split_bf16(w_router, 2)), b_router, set_of_row,
                                    tm=tm_small)
        base, lens, c0, nc = _moe_plan(cnt)
        xs, cs = _moe_sort(h2, comb, base, lens)
        ys = _moe_experts(xs, cs, c0, nc, wg_bf, wu_bf, wd_bf, l)
        unsort = functools.partial(_moe_unsort, ys, comb, x, mods3, base, lens, set_of_row)
        if l + 1 < depth:
            x = unsort(row0=0, n_rows=n_ctx + n_lat)
        else:
            yp = unsort(row0=0, n_rows=n_ctx)
            ys_out = unsort(row0=n_ctx, n_rows=n_lat)

    return (yp.reshape(bc, tc, d), ys_out.reshape(bl, tl, d),
            new_kv[0].reshape(bc, depth, tc, N_HEADS, 2, SUB_W),
            new_kv[1].reshape(bc, depth, tc, N_HEADS, HEAD_W), new_st)
```

```python
import functools
import math

import jax
import jax.numpy as jnp
from jax import lax
from jax.experimental import pallas as pl
from jax.experimental.pallas import tpu as pltpu

F32 = jnp.float32
BF16 = jnp.bfloat16

EPS = 1e-6
LOG2E = 1.4426950408889634
GRID_W = 64
ROPE_BASE = 10000.0
HEAD_W = 128
SUB_W = 64
N_HEADS = 8
ATTN_KEY_BLOCK = 512
CTX_HEADS_PER_STEP = 4
RET_HEADS_PER_STEP = 2
CHUNK = 128
CONV_K = 31
CONV_HALO = 16
N_EXPERTS = 16
EXP_PER_GROUP = 4
N_GROUPS = 4
LANES = 128
NEG = -1e30
VMEM_LIMIT = 58 * 1024 * 1024

MOE_SUB = 256
MOE_TILE = 2048
MOE_CHUNK = 128
MOE_ROWS = MOE_TILE + N_GROUPS * MOE_CHUNK
SORT_WIN = MOE_SUB + 16
SORT_WIN_SMALL = MOE_SUB // 2
UNSORT_WIN = MOE_SUB + LANES
UNSORT_WIN_SMALL = MOE_SUB


def _cparams(sem):
    return pltpu.CompilerParams(dimension_semantics=sem, vmem_limit_bytes=VMEM_LIMIT)


def _lane(shape):
    return lax.broadcasted_iota(jnp.int32, shape, len(shape) - 1)


def _sigmoid(x):
    return 1.0 / (1.0 + jnp.exp(-x))


def _silu(x):
    return x * _sigmoid(x)


def _dot(a, b):
    return jnp.dot(a, b, preferred_element_type=F32)


def _dot_nt(a, b):
    return lax.dot_general(a, b, (((1,), (1,)), ((), ())), preferred_element_type=F32)


def _split_bf16(x, parts):
    out = []
    for _ in range(parts - 1):
        hi = x.astype(BF16)
        out.append(hi)
        x = x - hi.astype(F32)
    out.append(x.astype(BF16))
    return out


_ANY = pl.BlockSpec(memory_space=pl.ANY)
_ONCE = pl.Buffered(1)


def _mods_kernel(c_ref, w_ref, b_ref, o_ref):
    s = _silu(c_ref[...])
    o_ref[0] = jnp.dot(s, w_ref[0], preferred_element_type=F32,
                       precision=lax.Precision.HIGHEST) + b_ref[0]


def _mods(cond8, w_mod, b_mod):
    depth, d, n = w_mod.shape
    tn = 2048
    return pl.pallas_call(
        _mods_kernel,
        out_shape=jax.ShapeDtypeStruct((depth, 8, n), F32),
        grid=(depth, n // tn),
        in_specs=[pl.BlockSpec((8, d), lambda l, j: (0, 0)),
                  pl.BlockSpec((1, d, tn), lambda l, j: (l, 0, j)),
                  pl.BlockSpec((1, 1, tn), lambda l, j: (l, 0, j))],
        out_specs=pl.BlockSpec((1, 8, tn), lambda l, j: (l, 0, j)),
        compiler_params=_cparams(("parallel", "parallel")),
        name="ada_mods",
    )(cond8, w_mod, b_mod.reshape(depth, 1, n))


def _inproj_kernel(x_ref, g_ref, sh_ref, sc_ref, w_ref, o_ref, h_scr):
    @pl.when(pl.program_id(1) == 0)
    def _():
        x = x_ref[...]
        r = lax.rsqrt(jnp.mean(x * x, axis=-1, keepdims=True) + EPS)
        h = x * r * g_ref[...]
        h_scr[...] = (h * (1.0 + sc_ref[0]) + sh_ref[0]).astype(BF16)

    o_ref[...] = _dot(h_scr[...], w_ref[...]).astype(o_ref.dtype)


def _inproj(x, g, mods3, w_bf, layer, set_of_row, tm=1024, tn=1024):
    nt, d = x.shape
    n = w_bf.shape[2]
    return pl.pallas_call(
        _inproj_kernel,
        out_shape=jax.ShapeDtypeStruct((nt, n), BF16),
        grid=(nt // tm, n // tn),
        in_specs=[pl.BlockSpec((tm, d), lambda i, j: (i, 0)),
                  pl.BlockSpec((1, d), lambda i, j: (0, 0)),
                  pl.BlockSpec((1, 1, d), lambda i, j: (set_of_row(i * tm), 0, 0)),
                  pl.BlockSpec((1, 1, d), lambda i, j: (set_of_row(i * tm), 0, 1)),
                  pl.BlockSpec((None, d, tn), lambda i, j: (layer, 0, j))],
        out_specs=pl.BlockSpec((tm, tn), lambda i, j: (i, j)),
        scratch_shapes=[pltpu.VMEM((tm, d), BF16)],
        compiler_params=_cparams(("parallel", "arbitrary")),
        name="inproj",
    )(x, g, mods3, mods3, w_bf)


def _qk_norm(x, g):
    lo = _lane(x.shape) < SUB_W
    x2 = x * x
    s_lo = jnp.sum(jnp.where(lo, x2, 0.0), axis=-1, keepdims=True)
    s_hi = jnp.sum(jnp.where(lo, 0.0, x2), axis=-1, keepdims=True)
    r = jnp.where(lo, lax.rsqrt(s_lo / SUB_W + EPS), lax.rsqrt(s_hi / SUB_W + EPS))
    return x * r * g


def _rope(x, cos, sin_signed):
    first = (_lane(x.shape) % 32) < 16
    partner = jnp.where(first, pltpu.roll(x, LANES - 16, 1), pltpu.roll(x, 16, 1))
    return x * cos + partner * sin_signed


def _attn_kernel(*refs, hps, n_cache, emit_kv, n_alias, **kw):
    if hps == 1:
        return _attn_head_kernel(*refs, n_cache=n_cache, emit_kv=emit_kv, n_alias=n_alias, **kw)
    assert not n_cache
    refs = list(refs)
    n_out = 3 if emit_kv else 1
    first_out = len(refs) - 3 - n_out
    for hh in range(hps):
        hs = slice(hh * HEAD_W, (hh + 1) * HEAD_W)
        head = list(refs)
        for j in (4, 5, 6, *range(first_out, first_out + n_out)):
            head[j] = refs[j].at[:, hs]
        for j in (-3, -2, -1):
            head[j] = refs[j].at[hh]
        _attn_head_kernel(*head, n_cache=n_cache, emit_kv=emit_kv, n_alias=n_alias, **kw)


def _attn_head_kernel(*refs, tq, n_cache, rope, emit_kv, n_alias, lam_init):
    it = iter(refs)
    lam_ref, gq_ref, gk_ref, gs_ref = next(it), next(it), next(it), next(it)
    q_ref, k_ref, v_ref = next(it), next(it), next(it)
    if n_cache:
        ck_ref, cv_ref = next(it), next(it)
    if rope:
        cos_ref, sin_ref = next(it), next(it)
    for _ in range(n_alias):
        next(it)
    o_ref = next(it)
    if emit_kv:
        ak_ref, av_ref = next(it), next(it)
    kn_scr, v_scr, s_scr = next(it), next(it), next(it)

    qi = pl.program_id(2)

    @pl.when(qi == 0)
    def _():
        kn = _qk_norm(k_ref[...].astype(F32), gk_ref[...])
        if emit_kv:
            ak_ref[...] = kn
            av_ref[...] = v_ref[...].astype(F32)
        if rope:
            kn = _rope(kn, cos_ref[...], sin_ref[...])
        if n_cache:
            kn_scr[:n_cache, :] = ck_ref[...].astype(BF16)
            v_scr[:n_cache, :HEAD_W] = cv_ref[...].astype(BF16)
        kn_scr[n_cache:, :] = kn.astype(BF16)
        v_scr[n_cache:, :HEAD_W] = v_ref[...]
        v_scr[:, HEAD_W:] = jnp.ones((v_scr.shape[0], HEAD_W), BF16)

    lp = lam_ref[...]
    lam = (jnp.exp(jnp.sum(lp[0:1] * lp[1:2], axis=-1, keepdims=True))
           - jnp.exp(jnp.sum(lp[2:3] * lp[3:4], axis=-1, keepdims=True)) + lam_init)

    qn = _qk_norm(q_ref[...].astype(F32), gq_ref[...])
    if rope:
        r0 = pl.multiple_of(qi * tq, tq)
        qn = _rope(qn, cos_ref[pl.ds(r0, tq), :], sin_ref[pl.ds(r0, tq), :])
    qn = qn * (SUB_W ** -0.5 * LOG2E)
    lo = _lane(qn.shape) < SUB_W
    qs = (jnp.where(lo, qn, 0.0).astype(BF16), jnp.where(lo, 0.0, qn).astype(BF16))
    tk = kn_scr.shape[0]
    kb = math.gcd(tk, ATTN_KEY_BLOCK)
    nkb = tk // kb

    mx = [None, None]
    for j in range(nkb):
        kj = kn_scr[j * kb:(j + 1) * kb, :]
        for m in range(2):
            s = _dot_nt(qs[m], kj)
            s_scr[m, :, j * kb:(j + 1) * kb] = s
            for i in range(kb // LANES):
                blk = s[:, i * LANES:(i + 1) * LANES]
                mx[m] = blk if mx[m] is None else jnp.maximum(mx[m], blk)
    m_row = [jnp.max(mx[m], axis=-1, keepdims=True) for m in range(2)]

    acc = [jnp.zeros((tq, 2 * HEAD_W), F32), jnp.zeros((tq, 2 * HEAD_W), F32)]
    for j in range(nkb):
        vj = v_scr[j * kb:(j + 1) * kb, :]
        for m in range(2):
            p = jnp.exp2(s_scr[m, :, j * kb:(j + 1) * kb] - m_row[m])
            acc[m] = acc[m] + _dot(p.astype(BF16), vj)
    o = (acc[0][:, :HEAD_W] / acc[0][:, HEAD_W:]
         - lam * (acc[1][:, :HEAD_W] / acc[1][:, HEAD_W:]))
    o = o * lax.rsqrt(jnp.mean(o * o, axis=-1, keepdims=True) + EPS) * (gs_ref[...] * (1.0 - lam_init))
    o_ref[...] = o.astype(o_ref.dtype)


def _attention(z, lam_p, gq, gk, gs, *, layer, row0, n_seq, t, tq, lam_init, hps=1, cache=None,
               rope_tabs=None, oa_prev=None, kv_prev=None, kv_shape=None):
    n_cache = 0 if cache is None else cache[0].shape[2]
    emit_kv = kv_shape is not None
    nq = t // tq
    rb = row0 // t
    rq = row0 // tq
    hw = hps * HEAD_W
    kcol, vcol = N_HEADS // hps, 2 * N_HEADS // hps
    in_specs = [pl.BlockSpec((4, SUB_W), lambda b, h, i: (0, 0)),
                pl.BlockSpec((1, HEAD_W), lambda b, h, i: (0, 0)),
                pl.BlockSpec((1, HEAD_W), lambda b, h, i: (0, 0)),
                pl.BlockSpec((1, HEAD_W), lambda b, h, i: (0, 0)),
                pl.BlockSpec((tq, hw), lambda b, h, i: (rq + b * nq + i, h)),
                pl.BlockSpec((t, hw), lambda b, h, i: (rb + b, kcol + h)),
                pl.BlockSpec((t, hw), lambda b, h, i: (rb + b, vcol + h))]
    args = [lam_p, gq, gk, gs, z, z, z]
    if n_cache:
        cspec = pl.BlockSpec((None, None, n_cache, HEAD_W), lambda b, h, i: (b, layer, 0, h))
        in_specs += [cspec, cspec]
        args += [cache[0], cache[1]]
    if rope_tabs is not None:
        in_specs += [pl.BlockSpec((t, HEAD_W), lambda b, h, i: (0, 0)),
                     pl.BlockSpec((t, HEAD_W), lambda b, h, i: (0, 0))]
        args += list(rope_tabs)
    aliases = {}
    n_alias = 0
    if oa_prev is not None:
        aliases[len(args)] = 0
        in_specs.append(_ANY)
        args.append(oa_prev)
        n_alias += 1
    if kv_prev is not None:
        for j, a in enumerate(kv_prev):
            aliases[len(args)] = 1 + j
            in_specs.append(_ANY)
            args.append(a)
            n_alias += 1
    out_shape = [jax.ShapeDtypeStruct((z.shape[0], N_HEADS * HEAD_W), BF16)]
    out_specs = [pl.BlockSpec((tq, hw), lambda b, h, i: (rq + b * nq + i, h))]
    if emit_kv:
        out_shape += [jax.ShapeDtypeStruct(kv_shape, F32)] * 2
        out_specs += [pl.BlockSpec((None, None, t, hw), lambda b, h, i: (b, layer, 0, h))] * 2
    per_head = () if hps == 1 else (hps,)
    return pl.pallas_call(
        functools.partial(_attn_kernel, hps=hps, tq=tq, n_cache=n_cache,
                          rope=rope_tabs is not None, emit_kv=emit_kv, n_alias=n_alias,
                          lam_init=lam_init),
        out_shape=out_shape,
        grid=(n_seq, N_HEADS // hps, nq),
        in_specs=in_specs,
        out_specs=out_specs,
        scratch_shapes=[pltpu.VMEM(per_head + (n_cache + t, HEAD_W), BF16),
                        pltpu.VMEM(per_head + (n_cache + t, 2 * HEAD_W), BF16),
                        pltpu.VMEM(per_head + (2, tq, n_cache + t), F32)],
        input_output_aliases=aliases,
        compiler_params=_cparams(("parallel", "parallel", "arbitrary")),
        name="diff_attn_lat" if n_cache else "diff_attn_ctx",
    )(*args)


def _log_sigmoid(x):
    return jnp.minimum(x, 0.0) - jnp.log1p(jnp.exp(-jnp.abs(x)))


def _group_norm(o):
    mu = jnp.mean(o, axis=-1, keepdims=True)
    d = o - mu
    return d * lax.rsqrt(jnp.mean(d * d, axis=-1, keepdims=True) + EPS)


def _ret_kernel(*refs, n_chunks, has_s0, emit_state, n_alias):
    it = iter(refs)
    dlf_ref, dlb_ref = next(it), next(it)
    q_ref, k_ref, v_ref, gf_ref, gb_ref = (next(it) for _ in range(5))
    if has_s0:
        s0f_ref, s0b_ref = next(it), next(it)
    for _ in range(n_alias):
        next(it)
    y_ref = next(it)
    if emit_state:
        st_ref = next(it)
    of_scr, ob_scr, s_scr, tab_scr = next(it), next(it), next(it), next(it)

    c = CHUNK
    nh = RET_HEADS_PER_STEP
    row = lax.broadcasted_iota(jnp.int32, (c, c), 0).astype(F32)
    col = lax.broadcasted_iota(jnp.int32, (c, c), 1).astype(F32)
    diff = row - col
    k_scale = HEAD_W ** -0.5
    cdec = []
    for hh in range(nh):
        lgf = _log_sigmoid(dlf_ref[hh])
        lgb = _log_sigmoid(dlb_ref[hh])
        tab_scr[hh, 0, 0] = jnp.where(diff >= 0, jnp.exp(jnp.maximum(diff, 0.0) * lgf), 0.0) * k_scale
        tab_scr[hh, 0, 1] = jnp.exp((row + 1.0) * lgf)
        tab_scr[hh, 0, 2] = jnp.exp((c - 1.0 - row) * lgf) * k_scale
        tab_scr[hh, 1, 0] = jnp.where(diff <= 0, jnp.exp(jnp.maximum(-diff, 0.0) * lgb), 0.0) * k_scale
        tab_scr[hh, 1, 1] = jnp.exp((c - row) * lgb)
        tab_scr[hh, 1, 2] = jnp.exp(row * lgb) * k_scale
        cdec.append((jnp.exp(c * lgf), jnp.exp(c * lgb)))
        for dr in range(2):
            if has_s0:
                s_scr[hh, dr] = (s0f_ref, s0b_ref)[dr][hh]
            else:
                s_scr[hh, dr] = jnp.zeros((c, c), F32)

    chains = [(hh, dr) for hh in range(nh) for dr in range(2)]
    cols = [slice(hh * HEAD_W, (hh + 1) * HEAD_W) for hh, _ in chains]

    def step(i, carry):
        rows = [pl.multiple_of((i if dr == 0 else n_chunks - 1 - i) * c, c) for _, dr in chains]
        qc = [q_ref[pl.ds(r, c), hs] for r, hs in zip(rows, cols)]
        kc = [k_ref[pl.ds(r, c), hs] for r, hs in zip(rows, cols)]
        vc = [v_ref[pl.ds(r, c), hs] for r, hs in zip(rows, cols)]
        st = [s_scr[hh, dr] for hh, dr in chains]
        att = [(_dot_nt(q, k) * tab_scr[hh, dr, 0]).astype(BF16)
               for q, k, (hh, dr) in zip(qc, kc, chains)]
        cross = [_dot(q, s.astype(BF16)) * tab_scr[hh, dr, 1] for q, s, (hh, dr) in zip(qc, st, chains)]
        kd = [(k.astype(F32) * tab_scr[hh, dr, 2]).T.astype(BF16) for k, (hh, dr) in zip(kc, chains)]
        for n, (hh, dr) in enumerate(chains):
            (of_scr, ob_scr)[dr][pl.ds(rows[n], c), cols[n]] = _dot(att[n], vc[n]) + cross[n]
        for n, (hh, dr) in enumerate(chains):
            s_scr[hh, dr] = st[n] * cdec[hh][dr] + _dot(kd[n], vc[n])
        return carry

    lax.fori_loop(0, n_chunks, step, 0, unroll=2)

    def combine(ci, carry):
        r = pl.multiple_of(ci * c, c)
        o = [(of_scr, ob_scr)[dr][pl.ds(r, c), cols[n]] for n, (_, dr) in enumerate(chains)]
        g = [(gf_ref, gb_ref)[dr][pl.ds(r, c), cols[n]].astype(F32) for n, (_, dr) in enumerate(chains)]
        d = [x - jnp.mean(x, axis=-1, keepdims=True) for x in o]
        inv = [lax.rsqrt(jnp.mean(x * x, axis=-1, keepdims=True) + EPS) for x in d]
        t = [x * w * _silu(gate) for x, w, gate in zip(d, inv, g)]
        for hh in range(nh):
            y_ref[pl.ds(r, c), cols[2 * hh]] = (t[2 * hh] + t[2 * hh + 1]).astype(y_ref.dtype)
        return carry

    lax.fori_loop(0, n_chunks, combine, 0)
    if emit_state:
        for hh in range(nh):
            for dr in range(2):
                st_ref[dr, hh] = s_scr[hh, dr]


def _retention(z, decay16, *, layer, row0, n_seq, t, s0=None, y_prev=None, st_prev=None,
               st_shape=None):
    nh = RET_HEADS_PER_STEP
    hw = nh * HEAD_W
    rb = row0 // t
    hp = N_HEADS // nh
    emit_state = st_shape is not None
    col = lambda j: (lambda b, h: (rb + b, j * hp + h))
    in_specs = [pl.BlockSpec((nh, 1, 1), lambda b, h: (h, 0, 0)),
                pl.BlockSpec((nh, 1, 1), lambda b, h: (hp + h, 0, 0))]
    in_specs += [pl.BlockSpec((t, hw), col(j)) for j in (3, 4, 5, 6, 7)]
    args = [decay16, decay16, z, z, z, z, z]
    if s0 is not None:
        for dr in range(2):
            in_specs.append(pl.BlockSpec((None, None, None, nh, CHUNK, CHUNK),
                                         lambda b, h, dr=dr: (b, layer, dr, h, 0, 0)))
        args += [s0, s0]
    aliases = {}
    n_alias = 0
    if y_prev is not None:
        aliases[len(args)] = 0
        in_specs.append(_ANY)
        args.append(y_prev)
        n_alias += 1
    if st_prev is not None:
        aliases[len(args)] = 1
        in_specs.append(_ANY)
        args.append(st_prev)
        n_alias += 1
    out_shape = [jax.ShapeDtypeStruct((z.shape[0], N_HEADS * HEAD_W), BF16)]
    out_specs = [pl.BlockSpec((t, hw), lambda b, h: (rb + b, h))]
    if emit_state:
        out_shape.append(jax.ShapeDtypeStruct(st_shape, F32))
        out_specs.append(pl.BlockSpec((None, None, 2, nh, CHUNK, CHUNK),
                                      lambda b, h: (b, layer, 0, h, 0, 0)))
    return pl.pallas_call(
        functools.partial(_ret_kernel, n_chunks=t // CHUNK, has_s0=s0 is not None,
                          emit_state=emit_state, n_alias=n_alias),
        out_shape=out_shape,
        grid=(n_seq, hp),
        in_specs=in_specs,
        out_specs=out_specs,
        scratch_shapes=[pltpu.VMEM((t, hw), F32), pltpu.VMEM((t, hw), F32),
                        pltpu.VMEM((nh, 2, CHUNK, CHUNK), F32),
                        pltpu.VMEM((nh, 2, 3, CHUNK, CHUNK), F32)],
        input_output_aliases=aliases,
        compiler_params=_cparams(("parallel", "parallel")),
        name="retention_lat" if s0 is not None else "retention_ctx",
    )(*args)


def _conv_kernel(*refs, tt, nt, rblk, n_alias):
    (a_ref, g_ref, ap_ref, gp_ref, an_ref, gn_ref, w_ref, b_ref, lg_ref, lb_ref) = refs[:10]
    o_ref, u_scr, sh_scr, y_scr = refs[10 + n_alias:]
    ti = pl.program_id(1)
    halo = CONV_HALO
    cw = a_ref.shape[1]

    def glu(a, g):
        return a[...].astype(F32) * _sigmoid(g[...].astype(F32))

    u_scr[halo:halo + tt, :] = glu(a_ref, g_ref)
    u_scr[0:halo, :] = jnp.where(ti > 0, glu(ap_ref, gp_ref), 0.0)
    u_scr[halo + tt:, :] = jnp.where(ti < nt - 1, glu(an_ref, gn_ref), 0.0)

    ext = tt + 2 * halo
    for cb in range(cw // LANES):
        cs = slice(cb * LANES, (cb + 1) * LANES)
        x = u_scr[:, cs]
        sh_scr[0] = x
        for b in range(1, 8):
            sh_scr[b] = pltpu.roll(x, ext - b, 0)

        def rows(ri, carry):
            r = pl.multiple_of(ri * rblk, rblk)
            acc = jnp.zeros((rblk, LANES), F32)
            for k in range(CONV_K):
                off = halo - CONV_K // 2 + k
                acc = acc + sh_scr[off % 8, pl.ds(r + 8 * (off // 8), rblk), :] * w_ref[k:k + 1, cs]
            y_scr[pl.ds(r, rblk), cs] = acc + b_ref[:, cs]
            return carry

        lax.fori_loop(0, tt // rblk, rows, 0)

    y = y_scr[...]
    mu = jnp.mean(y, axis=-1, keepdims=True)
    d = y - mu
    yn = d * lax.rsqrt(jnp.mean(d * d, axis=-1, keepdims=True) + EPS) * lg_ref[...] + lb_ref[...]
    o_ref[...] = _silu(yn).astype(o_ref.dtype)


def _conv(z, w, b, lg, lb, *, row0, n_seq, t, tt, y_prev=None):
    cw = w.shape[1]
    nt = t // tt
    total_h = z.shape[0] // CONV_HALO
    acol = 8192 // cw
    r_t = row0 // tt
    r_h = row0 // CONV_HALO
    per_h = tt // CONV_HALO

    def cur(j):
        return lambda s, i: (r_t + s * nt + i, acol + j)

    def prev(j):
        return lambda s, i: (jnp.maximum(r_h + (s * nt + i) * per_h - 1, 0), acol + j)

    def nxt(j):
        return lambda s, i: (jnp.minimum(r_h + (s * nt + i + 1) * per_h, total_h - 1), acol + j)

    full = lambda s, i: (0, 0)
    in_specs = [pl.BlockSpec((tt, cw), cur(0)), pl.BlockSpec((tt, cw), cur(1)),
                pl.BlockSpec((CONV_HALO, cw), prev(0)), pl.BlockSpec((CONV_HALO, cw), prev(1)),
                pl.BlockSpec((CONV_HALO, cw), nxt(0)), pl.BlockSpec((CONV_HALO, cw), nxt(1)),
                pl.BlockSpec((CONV_K, cw), full), pl.BlockSpec((1, cw), full),
                pl.BlockSpec((1, cw), full), pl.BlockSpec((1, cw), full)]
    args = [z, z, z, z, z, z, w, b, lg, lb]
    aliases = {}
    if y_prev is not None:
        aliases[len(args)] = 0
        in_specs.append(_ANY)
        args.append(y_prev)
    return pl.pallas_call(
        functools.partial(_conv_kernel, tt=tt, nt=nt, rblk=64, n_alias=len(aliases)),
        out_shape=jax.ShapeDtypeStruct((z.shape[0], cw), BF16),
        grid=(n_seq, nt),
        in_specs=in_specs,
        out_specs=pl.BlockSpec((tt, cw), lambda s, i: (r_t + s * nt + i, 0)),
        scratch_shapes=[pltpu.VMEM((tt + 2 * CONV_HALO, cw), F32),
                        pltpu.VMEM((8, tt + 2 * CONV_HALO, LANES), F32),
                        pltpu.VMEM((tt, cw), F32)],
        input_output_aliases=aliases,
        compiler_params=_cparams(("parallel", "parallel")),
        name="conformer_conv",
    )(*args)


def _merge_kernel(a_ref, r_ref, c_ref, g0_ref, g1_ref, g2_ref, w_ref, o_ref):
    acc = _sigmoid(g0_ref[...].astype(F32)) * _dot(a_ref[...], w_ref[0])
    acc += _sigmoid(g1_ref[...].astype(F32)) * _dot(r_ref[...], w_ref[1])
    acc += _sigmoid(g2_ref[...].astype(F32)) * _dot(c_ref[...], w_ref[2])
    o_ref[...] = acc.astype(o_ref.dtype)


def _merge(oa, orr, oc, z, w_br, layer, tm=512, tn=1024):
    nt, bw = oa.shape
    d = w_br.shape[3]
    g0 = 10240 // tn
    gs = d // tn
    br = pl.BlockSpec((tm, bw), lambda i, j: (i, 0))
    return pl.pallas_call(
        _merge_kernel,
        out_shape=jax.ShapeDtypeStruct((nt, d), BF16),
        grid=(nt // tm, d // tn),
        in_specs=[br, br, br,
                  pl.BlockSpec((tm, tn), lambda i, j: (i, g0 + j)),
                  pl.BlockSpec((tm, tn), lambda i, j: (i, g0 + gs + j)),
                  pl.BlockSpec((tm, tn), lambda i, j: (i, g0 + 2 * gs + j)),
                  pl.BlockSpec((None, 3, bw, tn), lambda i, j: (layer, 0, 0, j))],
        out_specs=pl.BlockSpec((tm, tn), lambda i, j: (i, j)),
        compiler_params=_cparams(("parallel", "parallel")),
        name="branch_merge",
    )(oa, orr, oc, z, z, z, w_br)


def _outproj_kernel(m_ref, x_ref, w_ref, g1_ref, gn_ref, sh_ref, sc_ref, wr_ref, br_ref,
                    xo_ref, h_ref, cb_ref, cnt_ref):
    x = x_ref[...] + g1_ref[0] * _dot(m_ref[...], w_ref[...])
    xo_ref[...] = x
    h = x * lax.rsqrt(jnp.mean(x * x, axis=-1, keepdims=True) + EPS) * gn_ref[...]
    h = h * (1.0 + sc_ref[0]) + sh_ref[0]
    h_hi = h.astype(BF16)
    h_ref[...] = h_hi
    h_lo = (h - h_hi.astype(F32)).astype(BF16)

    hw = _dot(h_hi, wr_ref[...])
    lg = hw[:, :LANES] + hw[:, LANES:] + _dot(h_lo, wr_ref[:, :LANES]) + br_ref[...]
    lane = _lane(lg.shape).astype(F32)
    big = float(LANES)

    def first_max(v):
        m = jnp.max(v, axis=-1, keepdims=True)
        return m, jnp.min(jnp.where(v == m, lane, big), axis=-1, keepdims=True)

    is_g = (lane >= N_EXPERTS) & (lane < N_EXPERTS + N_GROUPS)
    gl = jnp.where(is_g, lg, NEG)
    gmax, gidx = first_max(gl)
    g_w = 1.0 / jnp.sum(jnp.exp(gl - gmax), axis=-1, keepdims=True)
    e_lo = (gidx - N_EXPERTS) * EXP_PER_GROUP
    is_e = (lane >= e_lo) & (lane < e_lo + EXP_PER_GROUP)
    el = jnp.where(is_e, lg, NEG)
    e1, i1 = first_max(el)
    e2, i2 = first_max(jnp.where(lane == i1, NEG, el))
    p2 = jnp.exp(e2 - e1)
    w1 = g_w / (1.0 + p2)
    onehot = jnp.where(lane == gidx, 1.0, 0.0)
    cb_ref[...] = jnp.where(lane == i1, w1, 0.0) + jnp.where(lane == i2, w1 * p2, 0.0) + onehot
    for s in range(cnt_ref.shape[0]):
        cnt_ref[s] = jnp.sum(onehot[s * MOE_SUB:(s + 1) * MOE_SUB], axis=0, keepdims=True)


def _outproj(merged, x, w_o, layer, mods3, g2, w_router, b_router, set_of_row, tm=512):
    nt, d = x.shape
    mod = lambda c: pl.BlockSpec((1, 1, d), lambda i: (set_of_row(i * tm), 0, c))
    full = lambda i: (0, 0)
    ns = tm // MOE_SUB
    return pl.pallas_call(
        _outproj_kernel,
        out_shape=[jax.ShapeDtypeStruct((nt, d), F32),
                   jax.ShapeDtypeStruct((nt, d), BF16),
                   jax.ShapeDtypeStruct((nt, LANES), F32),
                   jax.ShapeDtypeStruct((nt // MOE_SUB, 1, LANES), F32)],
        grid=(nt // tm,),
        in_specs=[pl.BlockSpec((tm, d), lambda i: (i, 0)),
                  pl.BlockSpec((tm, d), lambda i: (i, 0)),
                  pl.BlockSpec((None, d, d), lambda i: (layer, 0, 0)),
                  mod(2), pl.BlockSpec((1, d), full), mod(3), mod(4),
                  pl.BlockSpec((d, 2 * LANES), full), pl.BlockSpec((1, LANES), full)],
        out_specs=[pl.BlockSpec((tm, d), lambda i: (i, 0)),
                   pl.BlockSpec((tm, d), lambda i: (i, 0)),
                   pl.BlockSpec((tm, LANES), lambda i: (i, 0)),
                   pl.BlockSpec((ns, 1, LANES), lambda i: (i, 0, 0))],
        compiler_params=_cparams(("parallel",)),
        name="outproj_router",
    )(merged, x, w_o, mods3, g2, mods3, mods3, w_router, b_router)


def _group_onehot(cb):
    lane = _lane(cb.shape)
    return jnp.where((lane >= N_EXPERTS) & (lane < N_EXPERTS + N_GROUPS), cb, 0.0)


def _window_start(base, win):
    w0 = jnp.minimum((base // 16) * 16, MOE_ROWS - win)
    return pl.multiple_of(w0, 16)


def _moe_sort_kernel(base_ref, len_ref, h_ref, cb_ref, xs_ref, cs_ref, acc_scr):
    i, s = pl.program_id(0), pl.program_id(1)
    n_sub = pl.num_programs(1)

    @pl.when(s == 0)
    def _():
        acc_scr[...] = jnp.zeros_like(acc_scr)
        cs_ref[...] = jnp.zeros_like(cs_ref)

    cb = cb_ref[...]
    g1h = _group_onehot(cb).astype(BF16)
    eye = (lax.broadcasted_iota(jnp.int32, (LANES, LANES), 0)
           == lax.broadcasted_iota(jnp.int32, (LANES, LANES), 1)).astype(BF16)
    gt = _dot_nt(eye, g1h)
    before = (lax.broadcasted_iota(jnp.int32, (MOE_SUB, MOE_SUB), 0)
              < lax.broadcasted_iota(jnp.int32, (MOE_SUB, MOE_SUB), 1)).astype(BF16)
    rank_t = _dot(gt.astype(BF16), before)
    sub = lax.broadcasted_iota(jnp.int32, (LANES, 1), 0)
    bases = [base_ref[(i * n_sub + s) * N_GROUPS + g] for g in range(N_GROUPS)]
    basec = jnp.zeros((LANES, 1), F32)
    for g in range(N_GROUPS):
        basec = jnp.where(sub == N_EXPERTS + g, bases[g].astype(F32), basec)
    dest = jnp.sum(gt * (basec + rank_t), axis=0, keepdims=True)
    cparts = _split_bf16(cb, 3)
    lens = [len_ref[(i * n_sub + s) * N_GROUPS + g] for g in range(N_GROUPS)]
    longest = functools.reduce(jnp.maximum, lens)

    def place(win):
        w0 = [_window_start(bases[g], win) for g in range(N_GROUPS)]
        perm = []
        for g in range(N_GROUPS):
            slot = lax.broadcasted_iota(jnp.int32, (win, MOE_SUB), 0).astype(F32) + w0[g].astype(F32)
            ing = gt[N_EXPERTS + g:N_EXPERTS + g + 1, :]
            perm.append(jnp.where((slot == dest) & (ing > 0.5), 1.0, 0.0).astype(BF16))
        perm = jnp.concatenate(perm, axis=0)
        xg = _dot(perm, h_ref[...])
        cg = _dot(perm, cparts[0]) + _dot(perm, cparts[1]) + _dot(perm, cparts[2])
        for g in range(N_GROUPS):
            acc_scr[pl.ds(w0[g], win), :] += xg[g * win:(g + 1) * win]
            cs_ref[pl.ds(w0[g], win), :] += cg[g * win:(g + 1) * win]

    @pl.when(longest <= SORT_WIN_SMALL - 16)
    def _():
        place(SORT_WIN_SMALL)

    @pl.when(longest > SORT_WIN_SMALL - 16)
    def _():
        place(SORT_WIN)

    @pl.when(s == n_sub - 1)
    def _():
        xs_ref[...] = acc_scr[...].astype(BF16)


def _moe_expert_kernel(c0_ref, nc_ref, xs_ref, cs_ref, wg_ref, wu_ref, wd_ref, ys_ref):
    i, e = pl.program_id(0), pl.program_id(1)
    g = e // EXP_PER_GROUP

    @pl.when(e == 0)
    def _():
        ys_ref[...] = jnp.zeros_like(ys_ref)

    c0 = c0_ref[i * N_GROUPS + g]
    nc = nc_ref[i * N_GROUPS + g]

    def run(ci, rows):
        r = pl.multiple_of(ci * MOE_CHUNK, MOE_CHUNK)
        x = xs_ref[pl.ds(r, rows), :]
        cw = cs_ref[pl.ds(r, rows), :]
        w_e = jnp.sum(jnp.where(_lane(cw.shape) == e, cw, 0.0), axis=-1, keepdims=True)
        act = _silu(_dot(x, wg_ref[0])) * _dot(x, wu_ref[0]) * w_e
        ys_ref[pl.ds(r, rows), :] += _dot(act.astype(BF16), wd_ref[0])

    def quad(k, carry):
        run(c0 + 4 * k, 4 * MOE_CHUNK)
        return carry

    lax.fori_loop(0, nc // 4, quad, 0)

    @pl.when(nc % 4 >= 2)
    def _():
        run(c0 + nc // 4 * 4, 2 * MOE_CHUNK)

    @pl.when(nc % 2 == 1)
    def _():
        run(c0 + nc - 1, MOE_CHUNK)


def _moe_unsort_kernel(base_ref, len_ref, ys_ref, cb_ref, x_ref, g2_ref, o_ref, *, tile0):
    i, s = pl.program_id(0), pl.program_id(2)
    n_sub = pl.num_programs(2)
    cb = cb_ref[...]
    g1h = _group_onehot(cb)
    after = (lax.broadcasted_iota(jnp.int32, (MOE_SUB, MOE_SUB), 1)
             < lax.broadcasted_iota(jnp.int32, (MOE_SUB, MOE_SUB), 0)).astype(BF16)
    rank = _dot(after, g1h.astype(BF16))
    lane = _lane((1, LANES))
    bases = [base_ref[((tile0 + i) * n_sub + s) * N_GROUPS + g] for g in range(N_GROUPS)]
    basev = jnp.zeros((1, LANES), F32)
    for g in range(N_GROUPS):
        basev = jnp.where(lane == N_EXPERTS + g, bases[g].astype(F32), basev)
    dest = jnp.sum(g1h * (basev + rank), axis=-1, keepdims=True)
    lens = [len_ref[((tile0 + i) * n_sub + s) * N_GROUPS + g] for g in range(N_GROUPS)]
    longest = functools.reduce(jnp.maximum, lens)

    def gather(win):
        acc = jnp.zeros(o_ref.shape, F32)
        for g in range(N_GROUPS):
            w0 = _window_start(bases[g], win)
            slot = lax.broadcasted_iota(jnp.int32, (MOE_SUB, win), 1).astype(F32) + w0.astype(F32)
            ing = jnp.sum(jnp.where(_lane(cb.shape) == N_EXPERTS + g, cb, 0.0), axis=-1, keepdims=True)
            perm_t = jnp.where((slot == dest) & (ing > 0.5), 1.0, 0.0).astype(BF16)
            acc += _dot(perm_t, ys_ref[pl.ds(w0, win), :].astype(BF16))
        o_ref[...] = x_ref[...] + g2_ref[0] * acc

    @pl.when(longest <= UNSORT_WIN_SMALL - 16)
    def _():
        gather(UNSORT_WIN_SMALL)

    @pl.when(longest > UNSORT_WIN_SMALL - 16)
    def _():
        gather(UNSORT_WIN)


def _moe_sort(h2, comb, base, lens):
    nt, d = h2.shape
    n_tiles, n_sub = nt // MOE_TILE, MOE_TILE // MOE_SUB
    return pl.pallas_call(
        _moe_sort_kernel,
        out_shape=[jax.ShapeDtypeStruct((n_tiles * MOE_ROWS, d), BF16),
                   jax.ShapeDtypeStruct((n_tiles * MOE_ROWS, LANES), F32)],
        grid_spec=pltpu.PrefetchScalarGridSpec(
            num_scalar_prefetch=2,
            grid=(n_tiles, n_sub),
            in_specs=[pl.BlockSpec((MOE_SUB, d), lambda i, s, b, n: (i * n_sub + s, 0)),
                      pl.BlockSpec((MOE_SUB, LANES), lambda i, s, b, n: (i * n_sub + s, 0))],
            out_specs=[pl.BlockSpec((MOE_ROWS, d), lambda i, s, b, n: (i, 0), pipeline_mode=_ONCE),
                       pl.BlockSpec((MOE_ROWS, LANES), lambda i, s, b, n: (i, 0), pipeline_mode=_ONCE)],
            scratch_shapes=[pltpu.VMEM((MOE_ROWS, d), F32)]),
        compiler_params=_cparams(("parallel", "arbitrary")),
        name="moe_sort",
    )(base, lens, h2, comb)


def _moe_experts(xs, cs, c0, nc, wg, wu, wd, layer):
    d = xs.shape[1]
    n_tiles = xs.shape[0] // MOE_ROWS
    _, ne, _, ff = wg.shape
    return pl.pallas_call(
        _moe_expert_kernel,
        out_shape=jax.ShapeDtypeStruct((n_tiles * MOE_ROWS, d), F32),
        grid_spec=pltpu.PrefetchScalarGridSpec(
            num_scalar_prefetch=2,
            grid=(n_tiles, ne),
            in_specs=[pl.BlockSpec((MOE_ROWS, d), lambda i, e, a, b: (i, 0), pipeline_mode=_ONCE),
                      pl.BlockSpec((MOE_ROWS, LANES), lambda i, e, a, b: (i, 0), pipeline_mode=_ONCE),
                      pl.BlockSpec((None, 1, d, ff), lambda i, e, a, b: (layer, e, 0, 0)),
                      pl.BlockSpec((None, 1, d, ff), lambda i, e, a, b: (layer, e, 0, 0)),
                      pl.BlockSpec((None, 1, ff, d), lambda i, e, a, b: (layer, e, 0, 0))],
            out_specs=pl.BlockSpec((MOE_ROWS, d), lambda i, e, a, b: (i, 0), pipeline_mode=_ONCE)),
        compiler_params=_cparams(("parallel", "arbitrary")),
        name="moe_experts",
    )(c0, nc, xs, cs, wg, wu, wd)


def _moe_unsort(ys, comb, x, mods3, base, lens, set_of_row, *, row0, n_rows, tn=1024):
    d = x.shape[1]
    n_sub = MOE_TILE // MOE_SUB
    tile0, sub0 = row0 // MOE_TILE, row0 // MOE_SUB
    return pl.pallas_call(
        functools.partial(_moe_unsort_kernel, tile0=tile0),
        out_shape=jax.ShapeDtypeStruct((n_rows, d), F32),
        grid_spec=pltpu.PrefetchScalarGridSpec(
            num_scalar_prefetch=2,
            grid=(n_rows // MOE_TILE, d // tn, n_sub),
            in_specs=[pl.BlockSpec((MOE_ROWS, tn), lambda i, j, s, b, n: (tile0 + i, j)),
                      pl.BlockSpec((MOE_SUB, LANES), lambda i, j, s, b, n: (sub0 + i * n_sub + s, 0)),
                      pl.BlockSpec((MOE_SUB, tn), lambda i, j, s, b, n: (sub0 + i * n_sub + s, j)),
                      pl.BlockSpec((1, 1, tn),
                                   lambda i, j, s, b, n: (set_of_row(row0 + i * MOE_TILE), 0,
                                                          5 * (d // tn) + j))],
            out_specs=pl.BlockSpec((MOE_SUB, tn), lambda i, j, s, b, n: (i * n_sub + s, j))),
        compiler_params=_cparams(("parallel", "parallel", "arbitrary")),
        name="moe_unsort",
    )(base, lens, ys, comb, x, mods3)


def _moe_plan(cnt):
    n_sub = MOE_TILE // MOE_SUB
    c = cnt[:, 0, N_EXPERTS:N_EXPERTS + N_GROUPS].astype(jnp.int32).reshape(-1, n_sub, N_GROUPS)
    seg = (jnp.sum(c, axis=1) + MOE_CHUNK - 1) // MOE_CHUNK * MOE_CHUNK
    seg_start = jnp.cumsum(seg, axis=1) - seg
    base = seg_start[:, None, :] + jnp.cumsum(c, axis=1) - c
    return (base.reshape(-1), c.reshape(-1), (seg_start // MOE_CHUNK).reshape(-1),
            (seg // MOE_CHUNK).reshape(-1))


def _rope_tables(t):
    n_freq = SUB_W // 4
    tok = jnp.arange(t)
    row = (tok // GRID_W).astype(F32)
    col = (tok % GRID_W).astype(F32)
    inv = ROPE_BASE ** (-jnp.arange(n_freq, dtype=F32) / n_freq)
    ar, ac = row[:, None] * inv, col[:, None] * inv
    cos = jnp.concatenate([jnp.cos(ar)] * 2 + [jnp.cos(ac)] * 2, axis=1)
    sin = jnp.concatenate([-jnp.sin(ar), jnp.sin(ar), -jnp.sin(ac), jnp.sin(ac)], axis=1)
    return jnp.tile(cos, (1, 2)), jnp.tile(sin, (1, 2))


def kernel(x_prompt, x_sample, cache_diff_k, cache_diff_v, state_ret, c, c_ctx, w_mod, b_mod,
           g_norm1, g_norm2, w_in, g_qnorm, g_knorm, lambda_qk, g_subln, ret_decay_logit, conv_w,
           conv_b, conv_ln_g, conv_ln_b, w_branch, w_out, w_router_group, b_router_group,
           w_router_expert, b_router_expert, w_exp_gate, w_exp_up, w_exp_down):
    bc, tc, d = x_prompt.shape
    bl, tl, _ = x_sample.shape
    depth = w_mod.shape[0]
    past = cache_diff_k.shape[2]
    n_ctx, n_lat = bc * tc, bl * tl
    assert bl + 1 <= 8 and n_ctx % tl == 0 and n_ctx % MOE_TILE == 0 and tl % MOE_TILE == 0

    def set_of_row(r):
        return jnp.where(r < n_ctx, 0, 1 + (r - n_ctx) // tl)

    tm_big = math.gcd(1024, math.gcd(n_ctx, tl))
    tm_small = math.gcd(512, tm_big)

    x = jnp.concatenate([x_prompt.reshape(n_ctx, d), x_sample.reshape(n_lat, d)], axis=0)
    cond8 = jnp.zeros((8, d), F32).at[0].set(c_ctx).at[1:1 + bl].set(c)
    mods = _mods(cond8, w_mod, b_mod)
    cos, sin = _rope_tables(tl)
    ck = cache_diff_k.reshape(bl, depth, past, N_HEADS * HEAD_W)
    cv = cache_diff_v.reshape(bl, depth, past, N_HEADS * HEAD_W)
    kv_shape = (bc, depth, tc, N_HEADS * HEAD_W)
    st_shape = (bc, depth, 2, N_HEADS, CHUNK, CHUNK)

    w_in_bf, w_br_bf, w_out_bf = w_in.astype(BF16), w_branch.astype(BF16), w_out.astype(BF16)
    wg_bf, wu_bf, wd_bf = w_exp_gate.astype(BF16), w_exp_up.astype(BF16), w_exp_down.astype(BF16)

    new_kv, new_st = None, None
    for l in range(depth):
        lam_init = 0.8 - 0.6 * math.exp(-0.3 * l)
        mods3 = mods[l].reshape(8, 1, 6 * d)
        gq = jnp.tile(g_qnorm[l], 2)[None]
        gk = jnp.tile(g_knorm[l], 2)[None]
        gs = g_subln[l][None]
        decay16 = ret_decay_logit[l].reshape(2 * N_HEADS, 1, 1)

        z = _inproj(x, g_norm1[l][None], mods3, w_in_bf, l, set_of_row, tm=tm_big)

        oa, ak, av = _attention(z, lambda_qk[l], gq, gk, gs, layer=l, row0=0, n_seq=bc, t=tc,
                                tq=tc, lam_init=lam_init, hps=CTX_HEADS_PER_STEP, kv_prev=new_kv,
                                kv_shape=kv_shape)
        new_kv = (ak, av)
        (oa,) = _attention(z, lambda_qk[l], gq, gk, gs, layer=l, row0=n_ctx, n_seq=bl, t=tl,
                           tq=math.gcd(512, tl), lam_init=lam_init, cache=(ck, cv),
                           rope_tabs=(cos, sin), oa_prev=oa)
        orr, new_st = _retention(z, decay16, layer=l, row0=0, n_seq=bc, t=tc, st_prev=new_st,
                                 st_shape=st_shape)
        (orr,) = _retention(z, decay16, layer=l, row0=n_ctx, n_seq=bl, t=tl, s0=state_ret,
                            y_prev=orr)
        cargs = (conv_w[l], conv_b[l][None], conv_ln_g[l][None], conv_ln_b[l][None])
        oc = _conv(z, *cargs, row0=0, n_seq=bc, t=tc, tt=math.gcd(512, tc))
        oc = _conv(z, *cargs, row0=n_ctx, n_seq=bl, t=tl, tt=math.gcd(512, tl), y_prev=oc)

        merged = _merge(oa, orr, oc, z, w_br_bf, l, tm=tm_big)

        w_router = jnp.zeros((d, LANES), F32).at[:, :N_EXPERTS].set(w_router_expert[l])
        w_router = w_router.at[:, N_EXPERTS:N_EXPERTS + N_GROUPS].set(w_router_group[l])
        b_router = jnp.zeros((1, LANES), F32).at[0, :N_EXPERTS].set(b_router_expert[l])
        b_router = b_router.at[0, N_EXPERTS:N_EXPERTS + N_GROUPS].set(b_router_group[l])
        x, h2, comb, cnt = _outproj(merged, x, w_out_bf, l, mods3, g_norm2[l][None],
                                    jnp.concatenate(_split_bf16(w_router, 2), axis=1), b_router,
                                    set_of_row,
                                    tm=tm_small)
        base, lens, c0, nc = _moe_plan(cnt)
        xs, cs = _moe_sort(h2, comb, base, lens)
        ys = _moe_experts(xs, cs, c0, nc, wg_bf, wu_bf, wd_bf, l)
        unsort = functools.partial(_moe_unsort, ys, comb, x, mods3, base, lens, set_of_row)
        if l + 1 < depth:
            x = unsort(row0=0, n_rows=n_ctx + n_lat)
        else:
            yp = unsort(row0=0, n_rows=n_ctx)
            ys_out = unsort(row0=n_ctx, n_rows=n_lat)

    return (yp.reshape(bc, tc, d), ys_out.reshape(bl, tl, d),
            new_kv[0].reshape(bc, depth, tc, N_HEADS, 2, SUB_W),
            new_kv[1].reshape(bc, depth, tc, N_HEADS, HEAD_W), new_st)
```

```python
import functools
import math

import jax
import jax.numpy as jnp
from jax import lax
from jax.experimental import pallas as pl
from jax.experimental.pallas import tpu as pltpu

F32 = jnp.float32
BF16 = jnp.bfloat16

EPS = 1e-6
LOG2E = 1.4426950408889634
GRID_W = 64
ROPE_BASE = 10000.0
HEAD_W = 128
SUB_W = 64
N_HEADS = 8
ATTN_KEY_BLOCK = 512
CTX_HEADS_PER_STEP = 4
LONG_SEQ = 1024
RET_HEADS_PER_STEP = 2
CHUNK = 128
CONV_K = 31
CONV_HALO = 16
N_EXPERTS = 16
EXP_PER_GROUP = 4
N_GROUPS = 4
LANES = 128
NEG = -1e30
VMEM_LIMIT = 58 * 1024 * 1024

MOE_SUB = 256
MOE_TILE = 2048
MOE_CHUNK = 128
MOE_ROWS = MOE_TILE + N_GROUPS * MOE_CHUNK
SORT_WIN = MOE_SUB + 16
SORT_WIN_SMALL = MOE_SUB // 2
UNSORT_WIN = MOE_SUB + LANES
UNSORT_WIN_SMALL = MOE_SUB


def _cparams(sem):
    return pltpu.CompilerParams(dimension_semantics=sem, vmem_limit_bytes=VMEM_LIMIT)


def _lane(shape):
    return lax.broadcasted_iota(jnp.int32, shape, len(shape) - 1)


def _sigmoid(x):
    return 1.0 / (1.0 + jnp.exp(-x))


def _silu(x):
    return x * _sigmoid(x)


def _dot(a, b):
    return jnp.dot(a, b, preferred_element_type=F32)


def _dot_nt(a, b):
    return lax.dot_general(a, b, (((1,), (1,)), ((), ())), preferred_element_type=F32)


def _split_bf16(x, parts):
    out = []
    for _ in range(parts - 1):
        hi = x.astype(BF16)
        out.append(hi)
        x = x - hi.astype(F32)
    out.append(x.astype(BF16))
    return out


_ANY = pl.BlockSpec(memory_space=pl.ANY)
_ONCE = pl.Buffered(1)


def _mods_kernel(c_ref, w_ref, b_ref, o_ref):
    s = _silu(c_ref[...])
    o_ref[0] = jnp.dot(s, w_ref[0], preferred_element_type=F32,
                       precision=lax.Precision.HIGHEST) + b_ref[0]


def _mods(cond8, w_mod, b_mod):
    depth, d, n = w_mod.shape
    tn = 2048
    return pl.pallas_call(
        _mods_kernel,
        out_shape=jax.ShapeDtypeStruct((depth, 8, n), F32),
        grid=(depth, n // tn),
        in_specs=[pl.BlockSpec((8, d), lambda l, j: (0, 0)),
                  pl.BlockSpec((1, d, tn), lambda l, j: (l, 0, j)),
                  pl.BlockSpec((1, 1, tn), lambda l, j: (l, 0, j))],
        out_specs=pl.BlockSpec((1, 8, tn), lambda l, j: (l, 0, j)),
        compiler_params=_cparams(("parallel", "parallel")),
        name="ada_mods",
    )(cond8, w_mod, b_mod.reshape(depth, 1, n))


def _inproj_kernel(x_ref, g_ref, sh_ref, sc_ref, w_ref, o_ref, h_scr):
    @pl.when(pl.program_id(1) == 0)
    def _():
        x = x_ref[...]
        r = lax.rsqrt(jnp.mean(x * x, axis=-1, keepdims=True) + EPS)
        h = x * r * g_ref[...]
        h_scr[...] = (h * (1.0 + sc_ref[0]) + sh_ref[0]).astype(BF16)

    o_ref[...] = _dot(h_scr[...], w_ref[...]).astype(o_ref.dtype)


def _inproj(x, g, mods3, w_bf, layer, set_of_row, tm=1024, tn=1024):
    nt, d = x.shape
    n = w_bf.shape[2]
    return pl.pallas_call(
        _inproj_kernel,
        out_shape=jax.ShapeDtypeStruct((nt, n), BF16),
        grid=(nt // tm, n // tn),
        in_specs=[pl.BlockSpec((tm, d), lambda i, j: (i, 0)),
                  pl.BlockSpec((1, d), lambda i, j: (0, 0)),
                  pl.BlockSpec((1, 1, d), lambda i, j: (set_of_row(i * tm), 0, 0)),
                  pl.BlockSpec((1, 1, d), lambda i, j: (set_of_row(i * tm), 0, 1)),
                  pl.BlockSpec((None, d, tn), lambda i, j: (layer, 0, j))],
        out_specs=pl.BlockSpec((tm, tn), lambda i, j: (i, j)),
        scratch_shapes=[pltpu.VMEM((tm, d), BF16)],
        compiler_params=_cparams(("parallel", "arbitrary")),
        name="inproj",
    )(x, g, mods3, mods3, w_bf)


def _qk_norm(x, g, on_mxu):
    if on_mxu:
        same = (lax.broadcasted_iota(jnp.int32, (HEAD_W, HEAD_W), 0) // SUB_W
                == lax.broadcasted_iota(jnp.int32, (HEAD_W, HEAD_W), 1) // SUB_W).astype(BF16)
        hi, lo = _split_bf16(x * x, 2)
        return x * lax.rsqrt((_dot(hi, same) + _dot(lo, same)) / SUB_W + EPS) * g
    lo = _lane(x.shape) < SUB_W
    x2 = x * x
    s_lo = jnp.sum(jnp.where(lo, x2, 0.0), axis=-1, keepdims=True)
    s_hi = jnp.sum(jnp.where(lo, 0.0, x2), axis=-1, keepdims=True)
    r = jnp.where(lo, lax.rsqrt(s_lo / SUB_W + EPS), lax.rsqrt(s_hi / SUB_W + EPS))
    return x * r * g


def _rope(x, cos, sin_signed):
    first = (_lane(x.shape) % 32) < 16
    partner = jnp.where(first, pltpu.roll(x, LANES - 16, 1), pltpu.roll(x, 16, 1))
    return x * cos + partner * sin_signed


def _attn_kernel(*refs, hps, n_cache, emit_kv, n_alias, **kw):
    if hps == 1:
        return _attn_head_kernel(*refs, n_cache=n_cache, emit_kv=emit_kv, n_alias=n_alias, **kw)
    assert not n_cache
    refs = list(refs)
    n_out = 3 if emit_kv else 1
    first_out = len(refs) - 3 - n_out
    for hh in range(hps):
        hs = slice(hh * HEAD_W, (hh + 1) * HEAD_W)
        head = list(refs)
        for j in (4, 5, 6, *range(first_out, first_out + n_out)):
            head[j] = refs[j].at[:, hs]
        for j in (-3, -2, -1):
            head[j] = refs[j].at[hh]
        _attn_head_kernel(*head, n_cache=n_cache, emit_kv=emit_kv, n_alias=n_alias, **kw)


def _attn_head_kernel(*refs, tq, n_cache, rope, emit_kv, n_alias, lam_init, norm_on_mxu):
    it = iter(refs)
    lam_ref, gq_ref, gk_ref, gs_ref = next(it), next(it), next(it), next(it)
    q_ref, k_ref, v_ref = next(it), next(it), next(it)
    if n_cache:
        ck_ref, cv_ref = next(it), next(it)
    if rope:
        cos_ref, sin_ref = next(it), next(it)
    for _ in range(n_alias):
        next(it)
    o_ref = next(it)
    if emit_kv:
        ak_ref, av_ref = next(it), next(it)
    kn_scr, v_scr, s_scr = next(it), next(it), next(it)

    qi = pl.program_id(2)

    @pl.when(qi == 0)
    def _():
        kn = _qk_norm(k_ref[...].astype(F32), gk_ref[...], norm_on_mxu)
        if emit_kv:
            ak_ref[...] = kn
            av_ref[...] = v_ref[...].astype(F32)
        if rope:
            kn = _rope(kn, cos_ref[...], sin_ref[...])
        if n_cache:
            kn_scr[:n_cache, :] = ck_ref[...].astype(BF16)
            v_scr[:n_cache, :HEAD_W] = cv_ref[...].astype(BF16)
        kn_scr[n_cache:, :] = kn.astype(BF16)
        v_scr[n_cache:, :HEAD_W] = v_ref[...]
        v_scr[:, HEAD_W:] = jnp.ones((v_scr.shape[0], HEAD_W), BF16)

    lp = lam_ref[...]
    lam = (jnp.exp(jnp.sum(lp[0:1] * lp[1:2], axis=-1, keepdims=True))
           - jnp.exp(jnp.sum(lp[2:3] * lp[3:4], axis=-1, keepdims=True)) + lam_init)

    qn = _qk_norm(q_ref[...].astype(F32), gq_ref[...], norm_on_mxu)
    if rope:
        r0 = pl.multiple_of(qi * tq, tq)
        qn = _rope(qn, cos_ref[pl.ds(r0, tq), :], sin_ref[pl.ds(r0, tq), :])
    qn = qn * (SUB_W ** -0.5 * LOG2E)
    lo = _lane(qn.shape) < SUB_W
    qs = (jnp.where(lo, qn, 0.0).astype(BF16), jnp.where(lo, 0.0, qn).astype(BF16))
    tk = kn_scr.shape[0]
    kb = math.gcd(tk, ATTN_KEY_BLOCK)
    nkb = tk // kb

    mx = [None, None]
    for j in range(nkb):
        kj = kn_scr[j * kb:(j + 1) * kb, :]
        for m in range(2):
            s = _dot_nt(qs[m], kj)
            s_scr[m, :, j * kb:(j + 1) * kb] = s
            for i in range(kb // LANES):
                blk = s[:, i * LANES:(i + 1) * LANES]
                mx[m] = blk if mx[m] is None else jnp.maximum(mx[m], blk)
    m_row = [jnp.max(mx[m], axis=-1, keepdims=True) for m in range(2)]

    acc = [jnp.zeros((tq, 2 * HEAD_W), F32), jnp.zeros((tq, 2 * HEAD_W), F32)]
    for j in range(nkb):
        vj = v_scr[j * kb:(j + 1) * kb, :]
        for m in range(2):
            p = jnp.exp2(s_scr[m, :, j * kb:(j + 1) * kb] - m_row[m])
            acc[m] = acc[m] + _dot(p.astype(BF16), vj)
    o = (acc[0][:, :HEAD_W] / acc[0][:, HEAD_W:]
         - lam * (acc[1][:, :HEAD_W] / acc[1][:, HEAD_W:]))
    o = o * lax.rsqrt(jnp.mean(o * o, axis=-1, keepdims=True) + EPS) * (gs_ref[...] * (1.0 - lam_init))
    o_ref[...] = o.astype(o_ref.dtype)


def _attention(z, lam_p, gq, gk, gs, *, layer, row0, n_seq, t, tq, lam_init, hps=1, cache=None,
               rope_tabs=None, oa_prev=None, kv_prev=None, kv_shape=None):
    n_cache = 0 if cache is None else cache[0].shape[2]
    emit_kv = kv_shape is not None
    nq = t // tq
    rb = row0 // t
    rq = row0 // tq
    hw = hps * HEAD_W
    kcol, vcol = N_HEADS // hps, 2 * N_HEADS // hps
    in_specs = [pl.BlockSpec((4, SUB_W), lambda b, h, i: (0, 0)),
                pl.BlockSpec((1, HEAD_W), lambda b, h, i: (0, 0)),
                pl.BlockSpec((1, HEAD_W), lambda b, h, i: (0, 0)),
                pl.BlockSpec((1, HEAD_W), lambda b, h, i: (0, 0)),
                pl.BlockSpec((tq, hw), lambda b, h, i: (rq + b * nq + i, h)),
                pl.BlockSpec((t, hw), lambda b, h, i: (rb + b, kcol + h)),
                pl.BlockSpec((t, hw), lambda b, h, i: (rb + b, vcol + h))]
    args = [lam_p, gq, gk, gs, z, z, z]
    if n_cache:
        cspec = pl.BlockSpec((None, None, n_cache, HEAD_W), lambda b, h, i: (b, layer, 0, h))
        in_specs += [cspec, cspec]
        args += [cache[0], cache[1]]
    if rope_tabs is not None:
        in_specs += [pl.BlockSpec((t, HEAD_W), lambda b, h, i: (0, 0)),
                     pl.BlockSpec((t, HEAD_W), lambda b, h, i: (0, 0))]
        args += list(rope_tabs)
    aliases = {}
    n_alias = 0
    if oa_prev is not None:
        aliases[len(args)] = 0
        in_specs.append(_ANY)
        args.append(oa_prev)
        n_alias += 1
    if kv_prev is not None:
        for j, a in enumerate(kv_prev):
            aliases[len(args)] = 1 + j
            in_specs.append(_ANY)
            args.append(a)
            n_alias += 1
    out_shape = [jax.ShapeDtypeStruct((z.shape[0], N_HEADS * HEAD_W), BF16)]
    out_specs = [pl.BlockSpec((tq, hw), lambda b, h, i: (rq + b * nq + i, h))]
    if emit_kv:
        out_shape += [jax.ShapeDtypeStruct(kv_shape, F32)] * 2
        out_specs += [pl.BlockSpec((None, None, t, hw), lambda b, h, i: (b, layer, 0, h))] * 2
    per_head = () if hps == 1 else (hps,)
    return pl.pallas_call(
        functools.partial(_attn_kernel, hps=hps, tq=tq, n_cache=n_cache,
                          rope=rope_tabs is not None, emit_kv=emit_kv, n_alias=n_alias,
                          lam_init=lam_init, norm_on_mxu=t >= LONG_SEQ),
        out_shape=out_shape,
        grid=(n_seq, N_HEADS // hps, nq),
        in_specs=in_specs,
        out_specs=out_specs,
        scratch_shapes=[pltpu.VMEM(per_head + (n_cache + t, HEAD_W), BF16),
                        pltpu.VMEM(per_head + (n_cache + t, 2 * HEAD_W), BF16),
                        pltpu.VMEM(per_head + (2, tq, n_cache + t), F32)],
        input_output_aliases=aliases,
        compiler_params=_cparams(("parallel", "parallel", "arbitrary")),
        name="diff_attn_lat" if n_cache else "diff_attn_ctx",
    )(*args)


def _log_sigmoid(x):
    return jnp.minimum(x, 0.0) - jnp.log1p(jnp.exp(-jnp.abs(x)))


def _group_norm(o):
    mu = jnp.mean(o, axis=-1, keepdims=True)
    d = o - mu
    return d * lax.rsqrt(jnp.mean(d * d, axis=-1, keepdims=True) + EPS)


def _ret_kernel(*refs, n_chunks, has_s0, emit_state, n_alias):
    it = iter(refs)
    dlf_ref, dlb_ref = next(it), next(it)
    q_ref, k_ref, v_ref, gf_ref, gb_ref = (next(it) for _ in range(5))
    if has_s0:
        s0f_ref, s0b_ref = next(it), next(it)
    for _ in range(n_alias):
        next(it)
    y_ref = next(it)
    if emit_state:
        st_ref = next(it)
    of_scr, ob_scr, s_scr, tab_scr = next(it), next(it), next(it), next(it)

    c = CHUNK
    nh = RET_HEADS_PER_STEP
    row = lax.broadcasted_iota(jnp.int32, (c, c), 0).astype(F32)
    col = lax.broadcasted_iota(jnp.int32, (c, c), 1).astype(F32)
    diff = row - col
    k_scale = HEAD_W ** -0.5
    cdec = []
    for hh in range(nh):
        lgf = _log_sigmoid(dlf_ref[hh])
        lgb = _log_sigmoid(dlb_ref[hh])
        tab_scr[hh, 0, 0] = jnp.where(diff >= 0, jnp.exp(jnp.maximum(diff, 0.0) * lgf), 0.0) * k_scale
        tab_scr[hh, 0, 1] = jnp.exp((row + 1.0) * lgf)
        tab_scr[hh, 0, 2] = jnp.exp((c - 1.0 - row) * lgf) * k_scale
        tab_scr[hh, 1, 0] = jnp.where(diff <= 0, jnp.exp(jnp.maximum(-diff, 0.0) * lgb), 0.0) * k_scale
        tab_scr[hh, 1, 1] = jnp.exp((c - row) * lgb)
        tab_scr[hh, 1, 2] = jnp.exp(row * lgb) * k_scale
        cdec.append((jnp.exp(c * lgf), jnp.exp(c * lgb)))
        for dr in range(2):
            if has_s0:
                s_scr[hh, dr] = (s0f_ref, s0b_ref)[dr][hh]
            else:
                s_scr[hh, dr] = jnp.zeros((c, c), F32)

    chains = [(hh, dr) for hh in range(nh) for dr in range(2)]
    cols = [slice(hh * HEAD_W, (hh + 1) * HEAD_W) for hh, _ in chains]

    def step(i, carry):
        rows = [pl.multiple_of((i if dr == 0 else n_chunks - 1 - i) * c, c) for _, dr in chains]
        qc = [q_ref[pl.ds(r, c), hs] for r, hs in zip(rows, cols)]
        kc = [k_ref[pl.ds(r, c), hs] for r, hs in zip(rows, cols)]
        vc = [v_ref[pl.ds(r, c), hs] for r, hs in zip(rows, cols)]
        st = [s_scr[hh, dr] for hh, dr in chains]
        att = [(_dot_nt(q, k) * tab_scr[hh, dr, 0]).astype(BF16)
               for q, k, (hh, dr) in zip(qc, kc, chains)]
        cross = [_dot(q, s.astype(BF16)) * tab_scr[hh, dr, 1] for q, s, (hh, dr) in zip(qc, st, chains)]
        kd = [(k.astype(F32) * tab_scr[hh, dr, 2]).T.astype(BF16) for k, (hh, dr) in zip(kc, chains)]
        for n, (hh, dr) in enumerate(chains):
            (of_scr, ob_scr)[dr][pl.ds(rows[n], c), cols[n]] = _dot(att[n], vc[n]) + cross[n]
        for n, (hh, dr) in enumerate(chains):
            s_scr[hh, dr] = st[n] * cdec[hh][dr] + _dot(kd[n], vc[n])
        return carry

    lax.fori_loop(0, n_chunks, step, 0, unroll=math.gcd(4, n_chunks))

    def combine(ci, carry):
        r = pl.multiple_of(ci * c, c)
        o = [(of_scr, ob_scr)[dr][pl.ds(r, c), cols[n]] for n, (_, dr) in enumerate(chains)]
        g = [(gf_ref, gb_ref)[dr][pl.ds(r, c), cols[n]].astype(F32) for n, (_, dr) in enumerate(chains)]
        d = [x - jnp.mean(x, axis=-1, keepdims=True) for x in o]
        inv = [lax.rsqrt(jnp.mean(x * x, axis=-1, keepdims=True) + EPS) for x in d]
        t = [x * w * _silu(gate) for x, w, gate in zip(d, inv, g)]
        for hh in range(nh):
            y_ref[pl.ds(r, c), cols[2 * hh]] = (t[2 * hh] + t[2 * hh + 1]).astype(y_ref.dtype)
        return carry

    lax.fori_loop(0, n_chunks, combine, 0)
    if emit_state:
        for hh in range(nh):
            for dr in range(2):
                st_ref[dr, hh] = s_scr[hh, dr]


def _retention(z, decay16, *, layer, row0, n_seq, t, s0=None, y_prev=None, st_prev=None,
               st_shape=None):
    nh = RET_HEADS_PER_STEP
    hw = nh * HEAD_W
    rb = row0 // t
    hp = N_HEADS // nh
    emit_state = st_shape is not None
    col = lambda j: (lambda b, h: (rb + b, j * hp + h))
    in_specs = [pl.BlockSpec((nh, 1, 1), lambda b, h: (h, 0, 0)),
                pl.BlockSpec((nh, 1, 1), lambda b, h: (hp + h, 0, 0))]
    in_specs += [pl.BlockSpec((t, hw), col(j)) for j in (3, 4, 5, 6, 7)]
    args = [decay16, decay16, z, z, z, z, z]
    if s0 is not None:
        for dr in range(2):
            in_specs.append(pl.BlockSpec((None, None, None, nh, CHUNK, CHUNK),
                                         lambda b, h, dr=dr: (b, layer, dr, h, 0, 0)))
        args += [s0, s0]
    aliases = {}
    n_alias = 0
    if y_prev is not None:
        aliases[len(args)] = 0
        in_specs.append(_ANY)
        args.append(y_prev)
        n_alias += 1
    if st_prev is not None:
        aliases[len(args)] = 1
        in_specs.append(_ANY)
        args.append(st_prev)
        n_alias += 1
    out_shape = [jax.ShapeDtypeStruct((z.shape[0], N_HEADS * HEAD_W), BF16)]
    out_specs = [pl.BlockSpec((t, hw), lambda b, h: (rb + b, h))]
    if emit_state:
        out_shape.append(jax.ShapeDtypeStruct(st_shape, F32))
        out_specs.append(pl.BlockSpec((None, None, 2, nh, CHUNK, CHUNK),
                                      lambda b, h: (b, layer, 0, h, 0, 0)))
    return pl.pallas_call(
        functools.partial(_ret_kernel, n_chunks=t // CHUNK, has_s0=s0 is not None,
                          emit_state=emit_state, n_alias=n_alias),
        out_shape=out_shape,
        grid=(n_seq, hp),
        in_specs=in_specs,
        out_specs=out_specs,
        scratch_shapes=[pltpu.VMEM((t, hw), F32), pltpu.VMEM((t, hw), F32),
                        pltpu.VMEM((nh, 2, CHUNK, CHUNK), F32),
                        pltpu.VMEM((nh, 2, 3, CHUNK, CHUNK), F32)],
        input_output_aliases=aliases,
        compiler_params=_cparams(("parallel", "parallel")),
        name="retention_lat" if s0 is not None else "retention_ctx",
    )(*args)


def _conv_kernel(*refs, tt, nt, rblk, n_alias):
    (a_ref, g_ref, ap_ref, gp_ref, an_ref, gn_ref, w_ref, b_ref, lg_ref, lb_ref) = refs[:10]
    o_ref, u_scr, sh_scr, y_scr = refs[10 + n_alias:]
    ti = pl.program_id(1)
    halo = CONV_HALO
    cw = a_ref.shape[1]

    def glu(a, g):
        return a[...].astype(F32) * _sigmoid(g[...].astype(F32))

    u_scr[halo:halo + tt, :] = glu(a_ref, g_ref)
    u_scr[0:halo, :] = jnp.where(ti > 0, glu(ap_ref, gp_ref), 0.0)
    u_scr[halo + tt:, :] = jnp.where(ti < nt - 1, glu(an_ref, gn_ref), 0.0)

    ext = tt + 2 * halo
    for cb in range(cw // LANES):
        cs = slice(cb * LANES, (cb + 1) * LANES)
        x = u_scr[:, cs]
        sh_scr[0] = x
        for b in range(1, 8):
            sh_scr[b] = pltpu.roll(x, ext - b, 0)

        def rows(ri, carry):
            r = pl.multiple_of(ri * rblk, rblk)
            acc = jnp.zeros((rblk, LANES), F32)
            for k in range(CONV_K):
                off = halo - CONV_K // 2 + k
                acc = acc + sh_scr[off % 8, pl.ds(r + 8 * (off // 8), rblk), :] * w_ref[k:k + 1, cs]
            y_scr[pl.ds(r, rblk), cs] = acc + b_ref[:, cs]
            return carry

        lax.fori_loop(0, tt // rblk, rows, 0)

    y = y_scr[...]
    mu = jnp.mean(y, axis=-1, keepdims=True)
    d = y - mu
    yn = d * lax.rsqrt(jnp.mean(d * d, axis=-1, keepdims=True) + EPS) * lg_ref[...] + lb_ref[...]
    o_ref[...] = _silu(yn).astype(o_ref.dtype)


def _conv(z, w, b, lg, lb, *, row0, n_seq, t, tt, y_prev=None):
    cw = w.shape[1]
    nt = t // tt
    total_h = z.shape[0] // CONV_HALO
    acol = 8192 // cw
    r_t = row0 // tt
    r_h = row0 // CONV_HALO
    per_h = tt // CONV_HALO

    def cur(j):
        return lambda s, i: (r_t + s * nt + i, acol + j)

    def prev(j):
        return lambda s, i: (jnp.maximum(r_h + (s * nt + i) * per_h - 1, 0), acol + j)

    def nxt(j):
        return lambda s, i: (jnp.minimum(r_h + (s * nt + i + 1) * per_h, total_h - 1), acol + j)

    full = lambda s, i: (0, 0)
    in_specs = [pl.BlockSpec((tt, cw), cur(0)), pl.BlockSpec((tt, cw), cur(1)),
                pl.BlockSpec((CONV_HALO, cw), prev(0)), pl.BlockSpec((CONV_HALO, cw), prev(1)),
                pl.BlockSpec((CONV_HALO, cw), nxt(0)), pl.BlockSpec((CONV_HALO, cw), nxt(1)),
                pl.BlockSpec((CONV_K, cw), full), pl.BlockSpec((1, cw), full),
                pl.BlockSpec((1, cw), full), pl.BlockSpec((1, cw), full)]
    args = [z, z, z, z, z, z, w, b, lg, lb]
    aliases = {}
    if y_prev is not None:
        aliases[len(args)] = 0
        in_specs.append(_ANY)
        args.append(y_prev)
    return pl.pallas_call(
        functools.partial(_conv_kernel, tt=tt, nt=nt, rblk=64, n_alias=len(aliases)),
        out_shape=jax.ShapeDtypeStruct((z.shape[0], cw), BF16),
        grid=(n_seq, nt),
        in_specs=in_specs,
        out_specs=pl.BlockSpec((tt, cw), lambda s, i: (r_t + s * nt + i, 0)),
        scratch_shapes=[pltpu.VMEM((tt + 2 * CONV_HALO, cw), F32),
                        pltpu.VMEM((8, tt + 2 * CONV_HALO, LANES), F32),
                        pltpu.VMEM((tt, cw), F32)],
        input_output_aliases=aliases,
        compiler_params=_cparams(("parallel", "parallel")),
        name="conformer_conv",
    )(*args)


def _merge_kernel(a_ref, r_ref, c_ref, g0_ref, g1_ref, g2_ref, w_ref, o_ref):
    acc = _sigmoid(g0_ref[...].astype(F32)) * _dot(a_ref[...], w_ref[0])
    acc += _sigmoid(g1_ref[...].astype(F32)) * _dot(r_ref[...], w_ref[1])
    acc += _sigmoid(g2_ref[...].astype(F32)) * _dot(c_ref[...], w_ref[2])
    o_ref[...] = acc.astype(o_ref.dtype)


def _merge(oa, orr, oc, z, w_br, layer, tm=512, tn=1024):
    nt, bw = oa.shape
    d = w_br.shape[3]
    g0 = 10240 // tn
    gs = d // tn
    br = pl.BlockSpec((tm, bw), lambda i, j: (i, 0))
    return pl.pallas_call(
        _merge_kernel,
        out_shape=jax.ShapeDtypeStruct((nt, d), BF16),
        grid=(nt // tm, d // tn),
        in_specs=[br, br, br,
                  pl.BlockSpec((tm, tn), lambda i, j: (i, g0 + j)),
                  pl.BlockSpec((tm, tn), lambda i, j: (i, g0 + gs + j)),
                  pl.BlockSpec((tm, tn), lambda i, j: (i, g0 + 2 * gs + j)),
                  pl.BlockSpec((None, 3, bw, tn), lambda i, j: (layer, 0, 0, j))],
        out_specs=pl.BlockSpec((tm, tn), lambda i, j: (i, j)),
        compiler_params=_cparams(("parallel", "parallel")),
        name="branch_merge",
    )(oa, orr, oc, z, z, z, w_br)


def _outproj_kernel(m_ref, x_ref, w_ref, g1_ref, gn_ref, sh_ref, sc_ref, wr_ref, br_ref,
                    xo_ref, h_ref, cb_ref, cnt_ref):
    x = x_ref[...] + g1_ref[0] * _dot(m_ref[...], w_ref[...])
    xo_ref[...] = x
    h = x * lax.rsqrt(jnp.mean(x * x, axis=-1, keepdims=True) + EPS) * gn_ref[...]
    h = h * (1.0 + sc_ref[0]) + sh_ref[0]
    h_hi = h.astype(BF16)
    h_ref[...] = h_hi
    h_lo = (h - h_hi.astype(F32)).astype(BF16)

    hw = _dot(h_hi, wr_ref[...])
    lg = hw[:, :LANES] + hw[:, LANES:] + _dot(h_lo, wr_ref[:, :LANES]) + br_ref[...]
    lane = _lane(lg.shape).astype(F32)
    big = float(LANES)

    def first_max(v):
        m = jnp.max(v, axis=-1, keepdims=True)
        return m, jnp.min(jnp.where(v == m, lane, big), axis=-1, keepdims=True)

    is_g = (lane >= N_EXPERTS) & (lane < N_EXPERTS + N_GROUPS)
    gl = jnp.where(is_g, lg, NEG)
    gmax, gidx = first_max(gl)
    g_w = 1.0 / jnp.sum(jnp.exp(gl - gmax), axis=-1, keepdims=True)
    e_lo = (gidx - N_EXPERTS) * EXP_PER_GROUP
    is_e = (lane >= e_lo) & (lane < e_lo + EXP_PER_GROUP)
    el = jnp.where(is_e, lg, NEG)
    e1, i1 = first_max(el)
    e2, i2 = first_max(jnp.where(lane == i1, NEG, el))
    p2 = jnp.exp(e2 - e1)
    w1 = g_w / (1.0 + p2)
    onehot = jnp.where(lane == gidx, 1.0, 0.0)
    cb_ref[...] = jnp.where(lane == i1, w1, 0.0) + jnp.where(lane == i2, w1 * p2, 0.0) + onehot
    for s in range(cnt_ref.shape[0]):
        cnt_ref[s] = jnp.sum(onehot[s * MOE_SUB:(s + 1) * MOE_SUB], axis=0, keepdims=True)


def _outproj(merged, x, w_o, layer, mods3, g2, w_router, b_router, set_of_row, tm=512):
    nt, d = x.shape
    mod = lambda c: pl.BlockSpec((1, 1, d), lambda i: (set_of_row(i * tm), 0, c))
    full = lambda i: (0, 0)
    ns = tm // MOE_SUB
    return pl.pallas_call(
        _outproj_kernel,
        out_shape=[jax.ShapeDtypeStruct((nt, d), F32),
                   jax.ShapeDtypeStruct((nt, d), BF16),
                   jax.ShapeDtypeStruct((nt, LANES), F32),
                   jax.ShapeDtypeStruct((nt // MOE_SUB, 1, LANES), F32)],
        grid=(nt // tm,),
        in_specs=[pl.BlockSpec((tm, d), lambda i: (i, 0)),
                  pl.BlockSpec((tm, d), lambda i: (i, 0)),
                  pl.BlockSpec((None, d, d), lambda i: (layer, 0, 0)),
                  mod(2), pl.BlockSpec((1, d), full), mod(3), mod(4),
                  pl.BlockSpec((d, 2 * LANES), full), pl.BlockSpec((1, LANES), full)],
        out_specs=[pl.BlockSpec((tm, d), lambda i: (i, 0)),
                   pl.BlockSpec((tm, d), lambda i: (i, 0)),
                   pl.BlockSpec((tm, LANES), lambda i: (i, 0)),
                   pl.BlockSpec((ns, 1, LANES), lambda i: (i, 0, 0))],
        compiler_params=_cparams(("parallel",)),
        name="outproj_router",
    )(merged, x, w_o, mods3, g2, mods3, mods3, w_router, b_router)


def _group_onehot(cb):
    lane = _lane(cb.shape)
    return jnp.where((lane >= N_EXPERTS) & (lane < N_EXPERTS + N_GROUPS), cb, 0.0)


def _window_start(base, win):
    w0 = jnp.minimum((base // 16) * 16, MOE_ROWS - win)
    return pl.multiple_of(w0, 16)


def _moe_sort_kernel(base_ref, len_ref, h_ref, cb_ref, xs_ref, cs_ref, acc_scr):
    i, s = pl.program_id(0), pl.program_id(1)
    n_sub = pl.num_programs(1)

    @pl.when(s == 0)
    def _():
        acc_scr[...] = jnp.zeros_like(acc_scr)
        cs_ref[...] = jnp.zeros_like(cs_ref)

    cb = cb_ref[...]
    g1h = _group_onehot(cb).astype(BF16)
    eye = (lax.broadcasted_iota(jnp.int32, (LANES, LANES), 0)
           == lax.broadcasted_iota(jnp.int32, (LANES, LANES), 1)).astype(BF16)
    gt = _dot_nt(eye, g1h)
    before = (lax.broadcasted_iota(jnp.int32, (MOE_SUB, MOE_SUB), 0)
              < lax.broadcasted_iota(jnp.int32, (MOE_SUB, MOE_SUB), 1)).astype(BF16)
    rank_t = _dot(gt.astype(BF16), before)
    sub = lax.broadcasted_iota(jnp.int32, (LANES, 1), 0)
    bases = [base_ref[(i * n_sub + s) * N_GROUPS + g] for g in range(N_GROUPS)]
    basec = jnp.zeros((LANES, 1), F32)
    for g in range(N_GROUPS):
        basec = jnp.where(sub == N_EXPERTS + g, bases[g].astype(F32), basec)
    dest = jnp.sum(gt * (basec + rank_t), axis=0, keepdims=True)
    cparts = _split_bf16(cb, 3)
    lens = [len_ref[(i * n_sub + s) * N_GROUPS + g] for g in range(N_GROUPS)]
    longest = functools.reduce(jnp.maximum, lens)

    def place(win):
        w0 = [_window_start(bases[g], win) for g in range(N_GROUPS)]
        perm = []
        for g in range(N_GROUPS):
            slot = lax.broadcasted_iota(jnp.int32, (win, MOE_SUB), 0).astype(F32) + w0[g].astype(F32)
            ing = gt[N_EXPERTS + g:N_EXPERTS + g + 1, :]
            perm.append(jnp.where((slot == dest) & (ing > 0.5), 1.0, 0.0).astype(BF16))
        perm = jnp.concatenate(perm, axis=0)
        xg = _dot(perm, h_ref[...])
        cg = _dot(perm, cparts[0]) + _dot(perm, cparts[1]) + _dot(perm, cparts[2])
        for g in range(N_GROUPS):
            acc_scr[pl.ds(w0[g], win), :] += xg[g * win:(g + 1) * win]
            cs_ref[pl.ds(w0[g], win), :] += cg[g * win:(g + 1) * win]

    @pl.when(longest <= SORT_WIN_SMALL - 16)
    def _():
        place(SORT_WIN_SMALL)

    @pl.when(longest > SORT_WIN_SMALL - 16)
    def _():
        place(SORT_WIN)

    @pl.when(s == n_sub - 1)
    def _():
        xs_ref[...] = acc_scr[...].astype(BF16)


def _moe_expert_kernel(c0_ref, nc_ref, xs_ref, cs_ref, wg_ref, wu_ref, wd_ref, ys_ref):
    i, e = pl.program_id(0), pl.program_id(1)
    g = e // EXP_PER_GROUP

    @pl.when(e == 0)
    def _():
        ys_ref[...] = jnp.zeros_like(ys_ref)

    c0 = c0_ref[i * N_GROUPS + g]
    nc = nc_ref[i * N_GROUPS + g]

    def run(ci, rows):
        r = pl.multiple_of(ci * MOE_CHUNK, MOE_CHUNK)
        x = xs_ref[pl.ds(r, rows), :]
        cw = cs_ref[pl.ds(r, rows), :]
        w_e = jnp.sum(jnp.where(_lane(cw.shape) == e, cw, 0.0), axis=-1, keepdims=True)
        act = _silu(_dot(x, wg_ref[0])) * _dot(x, wu_ref[0]) * w_e
        ys_ref[pl.ds(r, rows), :] += _dot(act.astype(BF16), wd_ref[0])

    def quad(k, carry):
        run(c0 + 4 * k, 4 * MOE_CHUNK)
        return carry

    lax.fori_loop(0, nc // 4, quad, 0)

    @pl.when(nc % 4 >= 2)
    def _():
        run(c0 + nc // 4 * 4, 2 * MOE_CHUNK)

    @pl.when(nc % 2 == 1)
    def _():
        run(c0 + nc - 1, MOE_CHUNK)


def _moe_unsort_kernel(base_ref, len_ref, ys_ref, cb_ref, x_ref, g2_ref, o_ref, *, tile0):
    i, s = pl.program_id(0), pl.program_id(2)
    n_sub = pl.num_programs(2)
    cb = cb_ref[...]
    g1h = _group_onehot(cb)
    after = (lax.broadcasted_iota(jnp.int32, (MOE_SUB, MOE_SUB), 1)
             < lax.broadcasted_iota(jnp.int32, (MOE_SUB, MOE_SUB), 0)).astype(BF16)
    rank = _dot(after, g1h.astype(BF16))
    lane = _lane((1, LANES))
    bases = [base_ref[((tile0 + i) * n_sub + s) * N_GROUPS + g] for g in range(N_GROUPS)]
    basev = jnp.zeros((1, LANES), F32)
    for g in range(N_GROUPS):
        basev = jnp.where(lane == N_EXPERTS + g, bases[g].astype(F32), basev)
    dest = jnp.sum(g1h * (basev + rank), axis=-1, keepdims=True)
    lens = [len_ref[((tile0 + i) * n_sub + s) * N_GROUPS + g] for g in range(N_GROUPS)]
    longest = functools.reduce(jnp.maximum, lens)

    def gather(win):
        acc = jnp.zeros(o_ref.shape, F32)
        for g in range(N_GROUPS):
            w0 = _window_start(bases[g], win)
            slot = lax.broadcasted_iota(jnp.int32, (MOE_SUB, win), 1).astype(F32) + w0.astype(F32)
            ing = jnp.sum(jnp.where(_lane(cb.shape) == N_EXPERTS + g, cb, 0.0), axis=-1, keepdims=True)
            perm_t = jnp.where((slot == dest) & (ing > 0.5), 1.0, 0.0).astype(BF16)
            acc += _dot(perm_t, ys_ref[pl.ds(w0, win), :].astype(BF16))
        o_ref[...] = x_ref[...] + g2_ref[0] * acc

    @pl.when(longest <= UNSORT_WIN_SMALL - 16)
    def _():
        gather(UNSORT_WIN_SMALL)

    @pl.when(longest > UNSORT_WIN_SMALL - 16)
    def _():
        gather(UNSORT_WIN)


def _moe_sort(h2, comb, base, lens):
    nt, d = h2.shape
    n_tiles, n_sub = nt // MOE_TILE, MOE_TILE // MOE_SUB
    return pl.pallas_call(
        _moe_sort_kernel,
        out_shape=[jax.ShapeDtypeStruct((n_tiles * MOE_ROWS, d), BF16),
                   jax.ShapeDtypeStruct((n_tiles * MOE_ROWS, LANES), F32)],
        grid_spec=pltpu.PrefetchScalarGridSpec(
            num_scalar_prefetch=2,
            grid=(n_tiles, n_sub),
            in_specs=[pl.BlockSpec((MOE_SUB, d), lambda i, s, b, n: (i * n_sub + s, 0)),
                      pl.BlockSpec((MOE_SUB, LANES), lambda i, s, b, n: (i * n_sub + s, 0))],
            out_specs=[pl.BlockSpec((MOE_ROWS, d), lambda i, s, b, n: (i, 0), pipeline_mode=_ONCE),
                       pl.BlockSpec((MOE_ROWS, LANES), lambda i, s, b, n: (i, 0), pipeline_mode=_ONCE)],
            scratch_shapes=[pltpu.VMEM((MOE_ROWS, d), F32)]),
        compiler_params=_cparams(("parallel", "arbitrary")),
        name="moe_sort",
    )(base, lens, h2, comb)


def _moe_experts(xs, cs, c0, nc, wg, wu, wd, layer):
    d = xs.shape[1]
    n_tiles = xs.shape[0] // MOE_ROWS
    _, ne, _, ff = wg.shape
    return pl.pallas_call(
        _moe_expert_kernel,
        out_shape=jax.ShapeDtypeStruct((n_tiles * MOE_ROWS, d), F32),
        grid_spec=pltpu.PrefetchScalarGridSpec(
            num_scalar_prefetch=2,
            grid=(n_tiles, ne),
            in_specs=[pl.BlockSpec((MOE_ROWS, d), lambda i, e, a, b: (i, 0), pipeline_mode=_ONCE),
                      pl.BlockSpec((MOE_ROWS, LANES), lambda i, e, a, b: (i, 0), pipeline_mode=_ONCE),
                      pl.BlockSpec((None, 1, d, ff), lambda i, e, a, b: (layer, e, 0, 0)),
                      pl.BlockSpec((None, 1, d, ff), lambda i, e, a, b: (layer, e, 0, 0)),
                      pl.BlockSpec((None, 1, ff, d), lambda i, e, a, b: (layer, e, 0, 0))],
            out_specs=pl.BlockSpec((MOE_ROWS, d), lambda i, e, a, b: (i, 0), pipeline_mode=_ONCE)),
        compiler_params=_cparams(("parallel", "arbitrary")),
        name="moe_experts",
    )(c0, nc, xs, cs, wg, wu, wd)


def _moe_unsort(ys, comb, x, mods3, base, lens, set_of_row, *, row0, n_rows, tn=1024):
    d = x.shape[1]
    n_sub = MOE_TILE // MOE_SUB
    tile0, sub0 = row0 // MOE_TILE, row0 // MOE_SUB
    return pl.pallas_call(
        functools.partial(_moe_unsort_kernel, tile0=tile0),
        out_shape=jax.ShapeDtypeStruct((n_rows, d), F32),
        grid_spec=pltpu.PrefetchScalarGridSpec(
            num_scalar_prefetch=2,
            grid=(n_rows // MOE_TILE, d // tn, n_sub),
            in_specs=[pl.BlockSpec((MOE_ROWS, tn), lambda i, j, s, b, n: (tile0 + i, j)),
                      pl.BlockSpec((MOE_SUB, LANES), lambda i, j, s, b, n: (sub0 + i * n_sub + s, 0)),
                      pl.BlockSpec((MOE_SUB, tn), lambda i, j, s, b, n: (sub0 + i * n_sub + s, j)),
                      pl.BlockSpec((1, 1, tn),
                                   lambda i, j, s, b, n: (set_of_row(row0 + i * MOE_TILE), 0,
                                                          5 * (d // tn) + j))],
            out_specs=pl.BlockSpec((MOE_SUB, tn), lambda i, j, s, b, n: (i * n_sub + s, j))),
        compiler_params=_cparams(("parallel", "parallel", "arbitrary")),
        name="moe_unsort",
    )(base, lens, ys, comb, x, mods3)


def _moe_plan(cnt):
    n_sub = MOE_TILE // MOE_SUB
    c = cnt[:, 0, N_EXPERTS:N_EXPERTS + N_GROUPS].astype(jnp.int32).reshape(-1, n_sub, N_GROUPS)
    seg = (jnp.sum(c, axis=1) + MOE_CHUNK - 1) // MOE_CHUNK * MOE_CHUNK
    seg_start = jnp.cumsum(seg, axis=1) - seg
    base = seg_start[:, None, :] + jnp.cumsum(c, axis=1) - c
    return (base.reshape(-1), c.reshape(-1), (seg_start // MOE_CHUNK).reshape(-1),
            (seg // MOE_CHUNK).reshape(-1))


def _rope_tables(t):
    n_freq = SUB_W // 4
    tok = jnp.arange(t)
    row = (tok // GRID_W).astype(F32)
    col = (tok % GRID_W).astype(F32)
    inv = ROPE_BASE ** (-jnp.arange(n_freq, dtype=F32) / n_freq)
    ar, ac = row[:, None] * inv, col[:, None] * inv
    cos = jnp.concatenate([jnp.cos(ar)] * 2 + [jnp.cos(ac)] * 2, axis=1)
    sin = jnp.concatenate([-jnp.sin(ar), jnp.sin(ar), -jnp.sin(ac), jnp.sin(ac)], axis=1)
    return jnp.tile(cos, (1, 2)), jnp.tile(sin, (1, 2))


def kernel(x_prompt, x_sample, cache_diff_k, cache_diff_v, state_ret, c, c_ctx, w_mod, b_mod,
           g_norm1, g_norm2, w_in, g_qnorm, g_knorm, lambda_qk, g_subln, ret_decay_logit, conv_w,
           conv_b, conv_ln_g, conv_ln_b, w_branch, w_out, w_router_group, b_router_group,
           w_router_expert, b_router_expert, w_exp_gate, w_exp_up, w_exp_down):
    bc, tc, d = x_prompt.shape
    bl, tl, _ = x_sample.shape
    depth = w_mod.shape[0]
    past = cache_diff_k.shape[2]
    n_ctx, n_lat = bc * tc, bl * tl
    assert bl + 1 <= 8 and n_ctx % tl == 0 and n_ctx % MOE_TILE == 0 and tl % MOE_TILE == 0

    def set_of_row(r):
        return jnp.where(r < n_ctx, 0, 1 + (r - n_ctx) // tl)

    tm_big = math.gcd(1024, math.gcd(n_ctx, tl))
    tm_small = math.gcd(512, tm_big)

    x = jnp.concatenate([x_prompt.reshape(n_ctx, d), x_sample.reshape(n_lat, d)], axis=0)
    cond8 = jnp.zeros((8, d), F32).at[0].set(c_ctx).at[1:1 + bl].set(c)
    mods = _mods(cond8, w_mod, b_mod)
    cos, sin = _rope_tables(tl)
    ck = cache_diff_k.reshape(bl, depth, past, N_HEADS * HEAD_W)
    cv = cache_diff_v.reshape(bl, depth, past, N_HEADS * HEAD_W)
    kv_shape = (bc, depth, tc, N_HEADS * HEAD_W)
    st_shape = (bc, depth, 2, N_HEADS, CHUNK, CHUNK)

    w_in_bf, w_br_bf, w_out_bf = w_in.astype(BF16), w_branch.astype(BF16), w_out.astype(BF16)
    wg_bf, wu_bf, wd_bf = w_exp_gate.astype(BF16), w_exp_up.astype(BF16), w_exp_down.astype(BF16)

    new_kv, new_st = None, None
    for l in range(depth):
        lam_init = 0.8 - 0.6 * math.exp(-0.3 * l)
        mods3 = mods[l].reshape(8, 1, 6 * d)
        gq = jnp.tile(g_qnorm[l], 2)[None]
        gk = jnp.tile(g_knorm[l], 2)[None]
        gs = g_subln[l][None]
        decay16 = ret_decay_logit[l].reshape(2 * N_HEADS, 1, 1)

        z = _inproj(x, g_norm1[l][None], mods3, w_in_bf, l, set_of_row, tm=tm_big)

        oa, ak, av = _attention(z, lambda_qk[l], gq, gk, gs, layer=l, row0=0, n_seq=bc, t=tc,
                                tq=tc, lam_init=lam_init, hps=CTX_HEADS_PER_STEP, kv_prev=new_kv,
                                kv_shape=kv_shape)
        new_kv = (ak, av)
        (oa,) = _attention(z, lambda_qk[l], gq, gk, gs, layer=l, row0=n_ctx, n_seq=bl, t=tl,
                           tq=math.gcd(512, tl), lam_init=lam_init, cache=(ck, cv),
                           rope_tabs=(cos, sin), oa_prev=oa)
        orr, new_st = _retention(z, decay16, layer=l, row0=0, n_seq=bc, t=tc, st_prev=new_st,
                                 st_shape=st_shape)
        (orr,) = _retention(z, decay16, layer=l, row0=n_ctx, n_seq=bl, t=tl, s0=state_ret,
                            y_prev=orr)
        cargs = (conv_w[l], conv_b[l][None], conv_ln_g[l][None], conv_ln_b[l][None])
        oc = _conv(z, *cargs, row0=0, n_seq=bc, t=tc, tt=math.gcd(512, tc))
        oc = _conv(z, *cargs, row0=n_ctx, n_seq=bl, t=tl, tt=math.gcd(512, tl), y_prev=oc)

        merged = _merge(oa, orr, oc, z, w_br_bf, l, tm=tm_big)

        w_router = jnp.zeros((d, LANES), F32).at[:, :N_EXPERTS].set(w_router_expert[l])
        w_router = w_router.at[:, N_EXPERTS:N_EXPERTS + N_GROUPS].set(w_router_group[l])
        b_router = jnp.zeros((1, LANES), F32).at[0, :N_EXPERTS].set(b_router_expert[l])
        b_router = b_router.at[0, N_EXPERTS:N_EXPERTS + N_GROUPS].set(b_router_group[l])
        x, h2, comb, cnt = _outproj(merged, x, w_out_bf, l, mods3, g_norm2[l][None],
                                    jnp.concatenate(_split_bf16(w_router, 2), axis=1), b_router,
                                    set_of_row,
                                    tm=tm_small)
        base, lens, c0, nc = _moe_plan(cnt)
        xs, cs = _moe_sort(h2, comb, base, lens)
        ys = _moe_experts(xs, cs, c0, nc, wg_bf, wu_bf, wd_bf, l)
        unsort = functools.partial(_moe_unsort, ys, comb, x, mods3, base, lens, set_of_row)
        if l + 1 < depth:
            x = unsort(row0=0, n_rows=n_ctx + n_lat)
        else:
            yp = unsort(row0=0, n_rows=n_ctx)
            ys_out = unsort(row0=n_ctx, n_rows=n_lat)

    return (yp.reshape(bc, tc, d), ys_out.reshape(bl, tl, d),
            new_kv[0].reshape(bc, depth, tc, N_HEADS, 2, SUB_W),
            new_kv[1].reshape(bc, depth, tc, N_HEADS, HEAD_W), new_st)
```

```python
import functools
import math

import jax
import jax.numpy as jnp
from jax import lax
from jax.experimental import pallas as pl
from jax.experimental.pallas import tpu as pltpu

F32 = jnp.float32
BF16 = jnp.bfloat16

EPS = 1e-6
LOG2E = 1.4426950408889634
GRID_W = 64
ROPE_BASE = 10000.0
HEAD_W = 128
SUB_W = 64
N_HEADS = 8
ATTN_KEY_BLOCK = 512
CTX_HEADS_PER_STEP = 4
LONG_SEQ = 1024
RET_HEADS_PER_STEP = 2
CHUNK = 128
CONV_K = 31
CONV_HALO = 16
N_EXPERTS = 16
EXP_PER_GROUP = 4
N_GROUPS = 4
LANES = 128
NEG = -1e30
VMEM_LIMIT = 58 * 1024 * 1024

MOE_SUB = 256
MOE_TILE = 2048
MOE_CHUNK = 128
MOE_ROWS = MOE_TILE + N_GROUPS * MOE_CHUNK
SORT_WIN = MOE_SUB + 16
SORT_WIN_SMALL = MOE_SUB // 2
UNSORT_WIN = MOE_SUB + LANES
UNSORT_WIN_SMALL = MOE_SUB


def _cparams(sem):
    return pltpu.CompilerParams(dimension_semantics=sem, vmem_limit_bytes=VMEM_LIMIT)


def _lane(shape):
    return lax.broadcasted_iota(jnp.int32, shape, len(shape) - 1)


def _sigmoid(x):
    return 1.0 / (1.0 + jnp.exp(-x))


def _silu(x):
    return x * _sigmoid(x)


def _dot(a, b):
    return jnp.dot(a, b, preferred_element_type=F32)


def _dot_nt(a, b):
    return lax.dot_general(a, b, (((1,), (1,)), ((), ())), preferred_element_type=F32)


def _split_bf16(x, parts):
    out = []
    for _ in range(parts - 1):
        hi = x.astype(BF16)
        out.append(hi)
        x = x - hi.astype(F32)
    out.append(x.astype(BF16))
    return out


_ANY = pl.BlockSpec(memory_space=pl.ANY)
_ONCE = pl.Buffered(1)


def _mods_kernel(c_ref, w_ref, b_ref, o_ref):
    s_hi, s_lo = _split_bf16(_silu(c_ref[...]), 2)
    w = w_ref[0].astype(BF16)
    o_ref[0] = _dot(s_hi, w) + _dot(s_lo, w) + b_ref[0]


def _mods(cond8, w_mod, b_mod):
    depth, d, n = w_mod.shape
    tn = 2048
    return pl.pallas_call(
        _mods_kernel,
        out_shape=jax.ShapeDtypeStruct((depth, 8, n), F32),
        grid=(depth, n // tn),
        in_specs=[pl.BlockSpec((8, d), lambda l, j: (0, 0)),
                  pl.BlockSpec((1, d, tn), lambda l, j: (l, 0, j)),
                  pl.BlockSpec((1, 1, tn), lambda l, j: (l, 0, j))],
        out_specs=pl.BlockSpec((1, 8, tn), lambda l, j: (l, 0, j)),
        compiler_params=_cparams(("parallel", "parallel")),
        name="ada_mods",
    )(cond8, w_mod, b_mod.reshape(depth, 1, n))


def _inproj_kernel(x_ref, g_ref, sh_ref, sc_ref, w_ref, o_ref, h_scr):
    @pl.when(pl.program_id(1) == 0)
    def _():
        x = x_ref[...]
        r = lax.rsqrt(jnp.mean(x * x, axis=-1, keepdims=True) + EPS)
        h = x * r * g_ref[...]
        h_scr[...] = (h * (1.0 + sc_ref[0]) + sh_ref[0]).astype(BF16)

    o_ref[...] = _dot(h_scr[...], w_ref[...]).astype(o_ref.dtype)


def _inproj(x, g, mods3, w_bf, layer, set_of_row, tm=1024, tn=1024):
    nt, d = x.shape
    n = w_bf.shape[2]
    return pl.pallas_call(
        _inproj_kernel,
        out_shape=jax.ShapeDtypeStruct((nt, n), BF16),
        grid=(nt // tm, n // tn),
        in_specs=[pl.BlockSpec((tm, d), lambda i, j: (i, 0)),
                  pl.BlockSpec((1, d), lambda i, j: (0, 0)),
                  pl.BlockSpec((1, 1, d), lambda i, j: (set_of_row(i * tm), 0, 0)),
                  pl.BlockSpec((1, 1, d), lambda i, j: (set_of_row(i * tm), 0, 1)),
                  pl.BlockSpec((None, d, tn), lambda i, j: (layer, 0, j))],
        out_specs=pl.BlockSpec((tm, tn), lambda i, j: (i, j)),
        scratch_shapes=[pltpu.VMEM((tm, d), BF16)],
        compiler_params=_cparams(("parallel", "arbitrary")),
        name="inproj",
    )(x, g, mods3, mods3, w_bf)


def _qk_norm(x, g, on_mxu):
    if on_mxu:
        same = (lax.broadcasted_iota(jnp.int32, (HEAD_W, HEAD_W), 0) // SUB_W
                == lax.broadcasted_iota(jnp.int32, (HEAD_W, HEAD_W), 1) // SUB_W).astype(BF16)
        hi, lo = _split_bf16(x * x, 2)
        return x * lax.rsqrt((_dot(hi, same) + _dot(lo, same)) / SUB_W + EPS) * g
    lo = _lane(x.shape) < SUB_W
    x2 = x * x
    s_lo = jnp.sum(jnp.where(lo, x2, 0.0), axis=-1, keepdims=True)
    s_hi = jnp.sum(jnp.where(lo, 0.0, x2), axis=-1, keepdims=True)
    r = jnp.where(lo, lax.rsqrt(s_lo / SUB_W + EPS), lax.rsqrt(s_hi / SUB_W + EPS))
    return x * r * g


def _rope(x, cos, sin_signed):
    first = (_lane(x.shape) % 32) < 16
    partner = jnp.where(first, pltpu.roll(x, LANES - 16, 1), pltpu.roll(x, 16, 1))
    return x * cos + partner * sin_signed


def _attn_kernel(*refs, hps, n_cache, emit_kv, n_alias, **kw):
    if hps == 1:
        return _attn_head_kernel(*refs, n_cache=n_cache, emit_kv=emit_kv, n_alias=n_alias, **kw)
    assert not n_cache
    refs = list(refs)
    n_out = 3 if emit_kv else 1
    first_out = len(refs) - 3 - n_out
    for hh in range(hps):
        hs = slice(hh * HEAD_W, (hh + 1) * HEAD_W)
        head = list(refs)
        for j in (4, 5, 6, *range(first_out, first_out + n_out)):
            head[j] = refs[j].at[:, hs]
        for j in (-3, -2, -1):
            head[j] = refs[j].at[hh]
        _attn_head_kernel(*head, n_cache=n_cache, emit_kv=emit_kv, n_alias=n_alias, **kw)


def _attn_head_kernel(*refs, tq, n_cache, rope, emit_kv, n_alias, lam_init, norm_on_mxu):
    it = iter(refs)
    lam_ref, gq_ref, gk_ref, gs_ref = next(it), next(it), next(it), next(it)
    q_ref, k_ref, v_ref = next(it), next(it), next(it)
    if n_cache:
        ck_ref, cv_ref = next(it), next(it)
    if rope:
        cos_ref, sin_ref = next(it), next(it)
    for _ in range(n_alias):
        next(it)
    o_ref = next(it)
    if emit_kv:
        ak_ref, av_ref = next(it), next(it)
    kn_scr, v_scr, s_scr = next(it), next(it), next(it)

    qi = pl.program_id(2)

    @pl.when(qi == 0)
    def _():
        kn = _qk_norm(k_ref[...].astype(F32), gk_ref[...], norm_on_mxu)
        if emit_kv:
            ak_ref[...] = kn
            av_ref[...] = v_ref[...].astype(F32)
        if rope:
            kn = _rope(kn, cos_ref[...], sin_ref[...])
        if n_cache:
            kn_scr[:n_cache, :] = ck_ref[...].astype(BF16)
            v_scr[:n_cache, :HEAD_W] = cv_ref[...].astype(BF16)
        kn_scr[n_cache:, :] = kn.astype(BF16)
        v_scr[n_cache:, :HEAD_W] = v_ref[...]
        v_scr[:, HEAD_W:] = jnp.ones((v_scr.shape[0], HEAD_W), BF16)

    lp = lam_ref[...]
    lam = (jnp.exp(jnp.sum(lp[0:1] * lp[1:2], axis=-1, keepdims=True))
           - jnp.exp(jnp.sum(lp[2:3] * lp[3:4], axis=-1, keepdims=True)) + lam_init)

    qn = _qk_norm(q_ref[...].astype(F32), gq_ref[...], norm_on_mxu)
    if rope:
        r0 = pl.multiple_of(qi * tq, tq)
        qn = _rope(qn, cos_ref[pl.ds(r0, tq), :], sin_ref[pl.ds(r0, tq), :])
    qn = qn * (SUB_W ** -0.5 * LOG2E)
    lo = _lane(qn.shape) < SUB_W
    qs = (jnp.where(lo, qn, 0.0).astype(BF16), jnp.where(lo, 0.0, qn).astype(BF16))
    tk = kn_scr.shape[0]
    kb = math.gcd(tk, ATTN_KEY_BLOCK)
    nkb = tk // kb

    mx = [None, None]
    for j in range(nkb):
        kj = kn_scr[j * kb:(j + 1) * kb, :]
        for m in range(2):
            s = _dot_nt(qs[m], kj)
            s_scr[m, :, j * kb:(j + 1) * kb] = s
            for i in range(kb // LANES):
                blk = s[:, i * LANES:(i + 1) * LANES]
                mx[m] = blk if mx[m] is None else jnp.maximum(mx[m], blk)
    m_row = [jnp.max(mx[m], axis=-1, keepdims=True) for m in range(2)]

    acc = [jnp.zeros((tq, 2 * HEAD_W), F32), jnp.zeros((tq, 2 * HEAD_W), F32)]
    for j in range(nkb):
        vj = v_scr[j * kb:(j + 1) * kb, :]
        for m in range(2):
            p = jnp.exp2(s_scr[m, :, j * kb:(j + 1) * kb] - m_row[m])
            acc[m] = acc[m] + _dot(p.astype(BF16), vj)
    o = (acc[0][:, :HEAD_W] / acc[0][:, HEAD_W:]
         - lam * (acc[1][:, :HEAD_W] / acc[1][:, HEAD_W:]))
    o = o * lax.rsqrt(jnp.mean(o * o, axis=-1, keepdims=True) + EPS) * (gs_ref[...] * (1.0 - lam_init))
    o_ref[...] = o.astype(o_ref.dtype)


def _attention(z, lam_p, gq, gk, gs, *, layer, row0, n_seq, t, tq, lam_init, hps=1, cache=None,
               rope_tabs=None, oa_prev=None, kv_prev=None, kv_shape=None):
    n_cache = 0 if cache is None else cache[0].shape[2]
    emit_kv = kv_shape is not None
    nq = t // tq
    rb = row0 // t
    rq = row0 // tq
    hw = hps * HEAD_W
    kcol, vcol = N_HEADS // hps, 2 * N_HEADS // hps
    in_specs = [pl.BlockSpec((4, SUB_W), lambda b, h, i: (0, 0)),
                pl.BlockSpec((1, HEAD_W), lambda b, h, i: (0, 0)),
                pl.BlockSpec((1, HEAD_W), lambda b, h, i: (0, 0)),
                pl.BlockSpec((1, HEAD_W), lambda b, h, i: (0, 0)),
                pl.BlockSpec((tq, hw), lambda b, h, i: (rq + b * nq + i, h)),
                pl.BlockSpec((t, hw), lambda b, h, i: (rb + b, kcol + h)),
                pl.BlockSpec((t, hw), lambda b, h, i: (rb + b, vcol + h))]
    args = [lam_p, gq, gk, gs, z, z, z]
    if n_cache:
        cspec = pl.BlockSpec((None, None, n_cache, HEAD_W), lambda b, h, i: (b, layer, 0, h))
        in_specs += [cspec, cspec]
        args += [cache[0], cache[1]]
    if rope_tabs is not None:
        in_specs += [pl.BlockSpec((t, HEAD_W), lambda b, h, i: (0, 0)),
                     pl.BlockSpec((t, HEAD_W), lambda b, h, i: (0, 0))]
        args += list(rope_tabs)
    aliases = {}
    n_alias = 0
    if oa_prev is not None:
        aliases[len(args)] = 0
        in_specs.append(_ANY)
        args.append(oa_prev)
        n_alias += 1
    if kv_prev is not None:
        for j, a in enumerate(kv_prev):
            aliases[len(args)] = 1 + j
            in_specs.append(_ANY)
            args.append(a)
            n_alias += 1
    out_shape = [jax.ShapeDtypeStruct((z.shape[0], N_HEADS * HEAD_W), BF16)]
    out_specs = [pl.BlockSpec((tq, hw), lambda b, h, i: (rq + b * nq + i, h))]
    if emit_kv:
        out_shape += [jax.ShapeDtypeStruct(kv_shape, F32)] * 2
        out_specs += [pl.BlockSpec((None, None, t, hw), lambda b, h, i: (b, layer, 0, h))] * 2
    per_head = () if hps == 1 else (hps,)
    return pl.pallas_call(
        functools.partial(_attn_kernel, hps=hps, tq=tq, n_cache=n_cache,
                          rope=rope_tabs is not None, emit_kv=emit_kv, n_alias=n_alias,
                          lam_init=lam_init, norm_on_mxu=t >= LONG_SEQ),
        out_shape=out_shape,
        grid=(n_seq, N_HEADS // hps, nq),
        in_specs=in_specs,
        out_specs=out_specs,
        scratch_shapes=[pltpu.VMEM(per_head + (n_cache + t, HEAD_W), BF16),
                        pltpu.VMEM(per_head + (n_cache + t, 2 * HEAD_W), BF16),
                        pltpu.VMEM(per_head + (2, tq, n_cache + t), F32)],
        input_output_aliases=aliases,
        compiler_params=_cparams(("parallel", "parallel", "arbitrary")),
        name="diff_attn_lat" if n_cache else "diff_attn_ctx",
    )(*args)


def _log_sigmoid(x):
    return jnp.minimum(x, 0.0) - jnp.log1p(jnp.exp(-jnp.abs(x)))


def _group_norm(o):
    mu = jnp.mean(o, axis=-1, keepdims=True)
    d = o - mu
    return d * lax.rsqrt(jnp.mean(d * d, axis=-1, keepdims=True) + EPS)


def _ret_kernel(*refs, n_chunks, has_s0, emit_state, n_alias):
    it = iter(refs)
    dlf_ref, dlb_ref = next(it), next(it)
    q_ref, k_ref, v_ref, gf_ref, gb_ref = (next(it) for _ in range(5))
    if has_s0:
        s0f_ref, s0b_ref = next(it), next(it)
    for _ in range(n_alias):
        next(it)
    y_ref = next(it)
    if emit_state:
        st_ref = next(it)
    of_scr, ob_scr, s_scr, tab_scr = next(it), next(it), next(it), next(it)

    c = CHUNK
    nh = RET_HEADS_PER_STEP
    row = lax.broadcasted_iota(jnp.int32, (c, c), 0).astype(F32)
    col = lax.broadcasted_iota(jnp.int32, (c, c), 1).astype(F32)
    diff = row - col
    k_scale = HEAD_W ** -0.5
    cdec = []
    for hh in range(nh):
        lgf = _log_sigmoid(dlf_ref[hh])
        lgb = _log_sigmoid(dlb_ref[hh])
        tab_scr[hh, 0, 0] = jnp.where(diff >= 0, jnp.exp(jnp.maximum(diff, 0.0) * lgf), 0.0) * k_scale
        tab_scr[hh, 0, 1] = jnp.exp((row + 1.0) * lgf)
        tab_scr[hh, 0, 2] = jnp.exp((c - 1.0 - row) * lgf) * k_scale
        tab_scr[hh, 1, 0] = jnp.where(diff <= 0, jnp.exp(jnp.maximum(-diff, 0.0) * lgb), 0.0) * k_scale
        tab_scr[hh, 1, 1] = jnp.exp((c - row) * lgb)
        tab_scr[hh, 1, 2] = jnp.exp(row * lgb) * k_scale
        cdec.append((jnp.exp(c * lgf), jnp.exp(c * lgb)))
        for dr in range(2):
            if has_s0:
                s_scr[hh, dr] = (s0f_ref, s0b_ref)[dr][hh]
            else:
                s_scr[hh, dr] = jnp.zeros((c, c), F32)

    chains = [(hh, dr) for hh in range(nh) for dr in range(2)]
    cols = [slice(hh * HEAD_W, (hh + 1) * HEAD_W) for hh, _ in chains]

    def step(i, carry):
        rows = [pl.multiple_of((i if dr == 0 else n_chunks - 1 - i) * c, c) for _, dr in chains]
        qc = [q_ref[pl.ds(r, c), hs] for r, hs in zip(rows, cols)]
        kc = [k_ref[pl.ds(r, c), hs] for r, hs in zip(rows, cols)]
        vc = [v_ref[pl.ds(r, c), hs] for r, hs in zip(rows, cols)]
        st = [s_scr[hh, dr] for hh, dr in chains]
        att = [(_dot_nt(q, k) * tab_scr[hh, dr, 0]).astype(BF16)
               for q, k, (hh, dr) in zip(qc, kc, chains)]
        cross = [_dot(q, s.astype(BF16)) * tab_scr[hh, dr, 1] for q, s, (hh, dr) in zip(qc, st, chains)]
        kd = [(k.astype(F32) * tab_scr[hh, dr, 2]).T.astype(BF16) for k, (hh, dr) in zip(kc, chains)]
        for n, (hh, dr) in enumerate(chains):
            (of_scr, ob_scr)[dr][pl.ds(rows[n], c), cols[n]] = _dot(att[n], vc[n]) + cross[n]
        for n, (hh, dr) in enumerate(chains):
            s_scr[hh, dr] = st[n] * cdec[hh][dr] + _dot(kd[n], vc[n])
        return carry

    lax.fori_loop(0, n_chunks, step, 0, unroll=math.gcd(4, n_chunks))

    def combine(ci, carry):
        r = pl.multiple_of(ci * c, c)
        o = [(of_scr, ob_scr)[dr][pl.ds(r, c), cols[n]] for n, (_, dr) in enumerate(chains)]
        g = [(gf_ref, gb_ref)[dr][pl.ds(r, c), cols[n]].astype(F32) for n, (_, dr) in enumerate(chains)]
        d = [x - jnp.mean(x, axis=-1, keepdims=True) for x in o]
        inv = [lax.rsqrt(jnp.mean(x * x, axis=-1, keepdims=True) + EPS) for x in d]
        t = [x * w * _silu(gate) for x, w, gate in zip(d, inv, g)]
        for hh in range(nh):
            y_ref[pl.ds(r, c), cols[2 * hh]] = (t[2 * hh] + t[2 * hh + 1]).astype(y_ref.dtype)
        return carry

    lax.fori_loop(0, n_chunks, combine, 0)
    if emit_state:
        for hh in range(nh):
            for dr in range(2):
                st_ref[dr, hh] = s_scr[hh, dr]


def _retention(z, decay16, *, layer, row0, n_seq, t, s0=None, y_prev=None, st_prev=None,
               st_shape=None):
    nh = RET_HEADS_PER_STEP
    hw = nh * HEAD_W
    rb = row0 // t
    hp = N_HEADS // nh
    emit_state = st_shape is not None
    col = lambda j: (lambda b, h: (rb + b, j * hp + h))
    in_specs = [pl.BlockSpec((nh, 1, 1), lambda b, h: (h, 0, 0)),
                pl.BlockSpec((nh, 1, 1), lambda b, h: (hp + h, 0, 0))]
    in_specs += [pl.BlockSpec((t, hw), col(j)) for j in (3, 4, 5, 6, 7)]
    args = [decay16, decay16, z, z, z, z, z]
    if s0 is not None:
        for dr in range(2):
            in_specs.append(pl.BlockSpec((None, None, None, nh, CHUNK, CHUNK),
                                         lambda b, h, dr=dr: (b, layer, dr, h, 0, 0)))
        args += [s0, s0]
    aliases = {}
    n_alias = 0
    if y_prev is not None:
        aliases[len(args)] = 0
        in_specs.append(_ANY)
        args.append(y_prev)
        n_alias += 1
    if st_prev is not None:
        aliases[len(args)] = 1
        in_specs.append(_ANY)
        args.append(st_prev)
        n_alias += 1
    out_shape = [jax.ShapeDtypeStruct((z.shape[0], N_HEADS * HEAD_W), BF16)]
    out_specs = [pl.BlockSpec((t, hw), lambda b, h: (rb + b, h))]
    if emit_state:
        out_shape.append(jax.ShapeDtypeStruct(st_shape, F32))
        out_specs.append(pl.BlockSpec((None, None, 2, nh, CHUNK, CHUNK),
                                      lambda b, h: (b, layer, 0, h, 0, 0)))
    return pl.pallas_call(
        functools.partial(_ret_kernel, n_chunks=t // CHUNK, has_s0=s0 is not None,
                          emit_state=emit_state, n_alias=n_alias),
        out_shape=out_shape,
        grid=(n_seq, hp),
        in_specs=in_specs,
        out_specs=out_specs,
        scratch_shapes=[pltpu.VMEM((t, hw), F32), pltpu.VMEM((t, hw), F32),
                        pltpu.VMEM((nh, 2, CHUNK, CHUNK), F32),
                        pltpu.VMEM((nh, 2, 3, CHUNK, CHUNK), F32)],
        input_output_aliases=aliases,
        compiler_params=_cparams(("parallel", "parallel")),
        name="retention_lat" if s0 is not None else "retention_ctx",
    )(*args)


def _conv_kernel(*refs, tt, nt, rblk, n_alias):
    (a_ref, g_ref, ap_ref, gp_ref, an_ref, gn_ref, w_ref, b_ref, lg_ref, lb_ref) = refs[:10]
    o_ref, u_scr, sh_scr, y_scr = refs[10 + n_alias:]
    ti = pl.program_id(1)
    halo = CONV_HALO
    cw = a_ref.shape[1]

    def glu(a, g):
        return a[...].astype(F32) * _sigmoid(g[...].astype(F32))

    u_scr[halo:halo + tt, :] = glu(a_ref, g_ref)
    u_scr[0:halo, :] = jnp.where(ti > 0, glu(ap_ref, gp_ref), 0.0)
    u_scr[halo + tt:, :] = jnp.where(ti < nt - 1, glu(an_ref, gn_ref), 0.0)

    ext = tt + 2 * halo
    for cb in range(cw // LANES):
        cs = slice(cb * LANES, (cb + 1) * LANES)
        x = u_scr[:, cs]
        sh_scr[0] = x
        for b in range(1, 8):
            sh_scr[b] = pltpu.roll(x, ext - b, 0)

        def rows(ri, carry):
            r = pl.multiple_of(ri * rblk, rblk)
            acc = jnp.zeros((rblk, LANES), F32)
            for k in range(CONV_K):
                off = halo - CONV_K // 2 + k
                acc = acc + sh_scr[off % 8, pl.ds(r + 8 * (off // 8), rblk), :] * w_ref[k:k + 1, cs]
            y_scr[pl.ds(r, rblk), cs] = acc + b_ref[:, cs]
            return carry

        lax.fori_loop(0, tt // rblk, rows, 0)

    y = y_scr[...]
    mu = jnp.mean(y, axis=-1, keepdims=True)
    d = y - mu
    yn = d * lax.rsqrt(jnp.mean(d * d, axis=-1, keepdims=True) + EPS) * lg_ref[...] + lb_ref[...]
    o_ref[...] = _silu(yn).astype(o_ref.dtype)


def _conv(z, w, b, lg, lb, *, row0, n_seq, t, tt, y_prev=None):
    cw = w.shape[1]
    nt = t // tt
    total_h = z.shape[0] // CONV_HALO
    acol = 8192 // cw
    r_t = row0 // tt
    r_h = row0 // CONV_HALO
    per_h = tt // CONV_HALO

    def cur(j):
        return lambda s, i: (r_t + s * nt + i, acol + j)

    def prev(j):
        return lambda s, i: (jnp.maximum(r_h + (s * nt + i) * per_h - 1, 0), acol + j)

    def nxt(j):
        return lambda s, i: (jnp.minimum(r_h + (s * nt + i + 1) * per_h, total_h - 1), acol + j)

    full = lambda s, i: (0, 0)
    in_specs = [pl.BlockSpec((tt, cw), cur(0)), pl.BlockSpec((tt, cw), cur(1)),
                pl.BlockSpec((CONV_HALO, cw), prev(0)), pl.BlockSpec((CONV_HALO, cw), prev(1)),
                pl.BlockSpec((CONV_HALO, cw), nxt(0)), pl.BlockSpec((CONV_HALO, cw), nxt(1)),
                pl.BlockSpec((CONV_K, cw), full), pl.BlockSpec((1, cw), full),
                pl.BlockSpec((1, cw), full), pl.BlockSpec((1, cw), full)]
    args = [z, z, z, z, z, z, w, b, lg, lb]
    aliases = {}
    if y_prev is not None:
        aliases[len(args)] = 0
        in_specs.append(_ANY)
        args.append(y_prev)
    return pl.pallas_call(
        functools.partial(_conv_kernel, tt=tt, nt=nt, rblk=64, n_alias=len(aliases)),
        out_shape=jax.ShapeDtypeStruct((z.shape[0], cw), BF16),
        grid=(n_seq, nt),
        in_specs=in_specs,
        out_specs=pl.BlockSpec((tt, cw), lambda s, i: (r_t + s * nt + i, 0)),
        scratch_shapes=[pltpu.VMEM((tt + 2 * CONV_HALO, cw), F32),
                        pltpu.VMEM((8, tt + 2 * CONV_HALO, LANES), F32),
                        pltpu.VMEM((tt, cw), F32)],
        input_output_aliases=aliases,
        compiler_params=_cparams(("parallel", "parallel")),
        name="conformer_conv",
    )(*args)


def _merge_kernel(a_ref, r_ref, c_ref, g0_ref, g1_ref, g2_ref, w_ref, o_ref):
    acc = _sigmoid(g0_ref[...].astype(F32)) * _dot(a_ref[...], w_ref[0])
    acc += _sigmoid(g1_ref[...].astype(F32)) * _dot(r_ref[...], w_ref[1])
    acc += _sigmoid(g2_ref[...].astype(F32)) * _dot(c_ref[...], w_ref[2])
    o_ref[...] = acc.astype(o_ref.dtype)


def _merge(oa, orr, oc, z, w_br, layer, tm=512, tn=1024):
    nt, bw = oa.shape
    d = w_br.shape[3]
    g0 = 10240 // tn
    gs = d // tn
    br = pl.BlockSpec((tm, bw), lambda i, j: (i, 0))
    return pl.pallas_call(
        _merge_kernel,
        out_shape=jax.ShapeDtypeStruct((nt, d), BF16),
        grid=(nt // tm, d // tn),
        in_specs=[br, br, br,
                  pl.BlockSpec((tm, tn), lambda i, j: (i, g0 + j)),
                  pl.BlockSpec((tm, tn), lambda i, j: (i, g0 + gs + j)),
                  pl.BlockSpec((tm, tn), lambda i, j: (i, g0 + 2 * gs + j)),
                  pl.BlockSpec((None, 3, bw, tn), lambda i, j: (layer, 0, 0, j))],
        out_specs=pl.BlockSpec((tm, tn), lambda i, j: (i, j)),
        compiler_params=_cparams(("parallel", "parallel")),
        name="branch_merge",
    )(oa, orr, oc, z, z, z, w_br)


def _outproj_kernel(m_ref, x_ref, w_ref, g1_ref, gn_ref, sh_ref, sc_ref, wr_ref, br_ref,
                    xo_ref, h_ref, cb_ref, cnt_ref):
    x = x_ref[...] + g1_ref[0] * _dot(m_ref[...], w_ref[...])
    xo_ref[...] = x
    h = x * lax.rsqrt(jnp.mean(x * x, axis=-1, keepdims=True) + EPS) * gn_ref[...]
    h = h * (1.0 + sc_ref[0]) + sh_ref[0]
    h_hi = h.astype(BF16)
    h_ref[...] = h_hi
    h_lo = (h - h_hi.astype(F32)).astype(BF16)

    hw = _dot(h_hi, wr_ref[...])
    lg = hw[:, :LANES] + hw[:, LANES:] + _dot(h_lo, wr_ref[:, :LANES]) + br_ref[...]
    lane = _lane(lg.shape).astype(F32)
    big = float(LANES)

    def first_max(v):
        m = jnp.max(v, axis=-1, keepdims=True)
        return m, jnp.min(jnp.where(v == m, lane, big), axis=-1, keepdims=True)

    is_g = (lane >= N_EXPERTS) & (lane < N_EXPERTS + N_GROUPS)
    gl = jnp.where(is_g, lg, NEG)
    gmax, gidx = first_max(gl)
    g_w = 1.0 / jnp.sum(jnp.exp(gl - gmax), axis=-1, keepdims=True)
    e_lo = (gidx - N_EXPERTS) * EXP_PER_GROUP
    is_e = (lane >= e_lo) & (lane < e_lo + EXP_PER_GROUP)
    el = jnp.where(is_e, lg, NEG)
    e1, i1 = first_max(el)
    e2, i2 = first_max(jnp.where(lane == i1, NEG, el))
    p2 = jnp.exp(e2 - e1)
    w1 = g_w / (1.0 + p2)
    onehot = jnp.where(lane == gidx, 1.0, 0.0)
    cb_ref[...] = jnp.where(lane == i1, w1, 0.0) + jnp.where(lane == i2, w1 * p2, 0.0) + onehot
    for s in range(cnt_ref.shape[0]):
        cnt_ref[s] = jnp.sum(onehot[s * MOE_SUB:(s + 1) * MOE_SUB], axis=0, keepdims=True)


def _outproj(merged, x, w_o, layer, mods3, g2, w_router, b_router, set_of_row, tm=512):
    nt, d = x.shape
    mod = lambda c: pl.BlockSpec((1, 1, d), lambda i: (set_of_row(i * tm), 0, c))
    full = lambda i: (0, 0)
    ns = tm // MOE_SUB
    return pl.pallas_call(
        _outproj_kernel,
        out_shape=[jax.ShapeDtypeStruct((nt, d), F32),
                   jax.ShapeDtypeStruct((nt, d), BF16),
                   jax.ShapeDtypeStruct((nt, LANES), F32),
                   jax.ShapeDtypeStruct((nt // MOE_SUB, 1, LANES), F32)],
        grid=(nt // tm,),
        in_specs=[pl.BlockSpec((tm, d), lambda i: (i, 0)),
                  pl.BlockSpec((tm, d), lambda i: (i, 0)),
                  pl.BlockSpec((None, d, d), lambda i: (layer, 0, 0)),
                  mod(2), pl.BlockSpec((1, d), full), mod(3), mod(4),
                  pl.BlockSpec((d, 2 * LANES), full), pl.BlockSpec((1, LANES), full)],
        out_specs=[pl.BlockSpec((tm, d), lambda i: (i, 0)),
                   pl.BlockSpec((tm, d), lambda i: (i, 0)),
                   pl.BlockSpec((tm, LANES), lambda i: (i, 0)),
                   pl.BlockSpec((ns, 1, LANES), lambda i: (i, 0, 0))],
        compiler_params=_cparams(("parallel",)),
        name="outproj_router",
    )(merged, x, w_o, mods3, g2, mods3, mods3, w_router, b_router)


def _group_onehot(cb):
    lane = _lane(cb.shape)
    return jnp.where((lane >= N_EXPERTS) & (lane < N_EXPERTS + N_GROUPS), cb, 0.0)


def _window_start(base, win):
    w0 = jnp.minimum((base // 16) * 16, MOE_ROWS - win)
    return pl.multiple_of(w0, 16)


def _moe_sort_kernel(base_ref, len_ref, h_ref, cb_ref, xs_ref, cs_ref, acc_scr):
    i, s = pl.program_id(0), pl.program_id(1)
    n_sub = pl.num_programs(1)

    @pl.when(s == 0)
    def _():
        acc_scr[...] = jnp.zeros_like(acc_scr)
        cs_ref[...] = jnp.zeros_like(cs_ref)

    cb = cb_ref[...]
    g1h = _group_onehot(cb).astype(BF16)
    eye = (lax.broadcasted_iota(jnp.int32, (LANES, LANES), 0)
           == lax.broadcasted_iota(jnp.int32, (LANES, LANES), 1)).astype(BF16)
    gt = _dot_nt(eye, g1h)
    before = (lax.broadcasted_iota(jnp.int32, (MOE_SUB, MOE_SUB), 0)
              < lax.broadcasted_iota(jnp.int32, (MOE_SUB, MOE_SUB), 1)).astype(BF16)
    rank_t = _dot(gt.astype(BF16), before)
    sub = lax.broadcasted_iota(jnp.int32, (LANES, 1), 0)
    bases = [base_ref[(i * n_sub + s) * N_GROUPS + g] for g in range(N_GROUPS)]
    basec = jnp.zeros((LANES, 1), F32)
    for g in range(N_GROUPS):
        basec = jnp.where(sub == N_EXPERTS + g, bases[g].astype(F32), basec)
    dest = jnp.sum(gt * (basec + rank_t), axis=0, keepdims=True)
    cparts = _split_bf16(cb, 3)
    lens = [len_ref[(i * n_sub + s) * N_GROUPS + g] for g in range(N_GROUPS)]
    longest = functools.reduce(jnp.maximum, lens)

    def place(win):
        w0 = [_window_start(bases[g], win) for g in range(N_GROUPS)]
        perm = []
        for g in range(N_GROUPS):
            slot = lax.broadcasted_iota(jnp.int32, (win, MOE_SUB), 0).astype(F32) + w0[g].astype(F32)
            ing = gt[N_EXPERTS + g:N_EXPERTS + g + 1, :]
            perm.append(jnp.where((slot == dest) & (ing > 0.5), 1.0, 0.0).astype(BF16))
        perm = jnp.concatenate(perm, axis=0)
        xg = _dot(perm, h_ref[...])
        cg = _dot(perm, cparts[0]) + _dot(perm, cparts[1]) + _dot(perm, cparts[2])
        for g in range(N_GROUPS):
            acc_scr[pl.ds(w0[g], win), :] += xg[g * win:(g + 1) * win]
            cs_ref[pl.ds(w0[g], win), :] += cg[g * win:(g + 1) * win]

    @pl.when(longest <= SORT_WIN_SMALL - 16)
    def _():
        place(SORT_WIN_SMALL)

    @pl.when(longest > SORT_WIN_SMALL - 16)
    def _():
        place(SORT_WIN)

    @pl.when(s == n_sub - 1)
    def _():
        xs_ref[...] = acc_scr[...].astype(BF16)


def _moe_expert_kernel(c0_ref, nc_ref, xs_ref, cs_ref, wg_ref, wu_ref, wd_ref, ys_ref):
    i, e = pl.program_id(0), pl.program_id(1)
    g = e // EXP_PER_GROUP

    @pl.when(e == 0)
    def _():
        ys_ref[...] = jnp.zeros_like(ys_ref)

    c0 = c0_ref[i * N_GROUPS + g]
    nc = nc_ref[i * N_GROUPS + g]

    def run(ci, rows):
        r = pl.multiple_of(ci * MOE_CHUNK, MOE_CHUNK)
        x = xs_ref[pl.ds(r, rows), :]
        cw = cs_ref[pl.ds(r, rows), :]
        w_e = jnp.sum(jnp.where(_lane(cw.shape) == e, cw, 0.0), axis=-1, keepdims=True)
        act = _silu(_dot(x, wg_ref[0])) * _dot(x, wu_ref[0]) * w_e
        ys_ref[pl.ds(r, rows), :] += _dot(act.astype(BF16), wd_ref[0])

    def quad(k, carry):
        run(c0 + 4 * k, 4 * MOE_CHUNK)
        return carry

    lax.fori_loop(0, nc // 4, quad, 0)

    @pl.when(nc % 4 >= 2)
    def _():
        run(c0 + nc // 4 * 4, 2 * MOE_CHUNK)

    @pl.when(nc % 2 == 1)
    def _():
        run(c0 + nc - 1, MOE_CHUNK)


def _moe_unsort_kernel(base_ref, len_ref, ys_ref, cb_ref, x_ref, g2_ref, o_ref, *, tile0):
    i, s = pl.program_id(0), pl.program_id(2)
    n_sub = pl.num_programs(2)
    cb = cb_ref[...]
    g1h = _group_onehot(cb)
    after = (lax.broadcasted_iota(jnp.int32, (MOE_SUB, MOE_SUB), 1)
             < lax.broadcasted_iota(jnp.int32, (MOE_SUB, MOE_SUB), 0)).astype(BF16)
    rank = _dot(after, g1h.astype(BF16))
    lane = _lane((1, LANES))
    bases = [base_ref[((tile0 + i) * n_sub + s) * N_GROUPS + g] for g in range(N_GROUPS)]
    basev = jnp.zeros((1, LANES), F32)
    for g in range(N_GROUPS):
        basev = jnp.where(lane == N_EXPERTS + g, bases[g].astype(F32), basev)
    dest = jnp.sum(g1h * (basev + rank), axis=-1, keepdims=True)
    lens = [len_ref[((tile0 + i) * n_sub + s) * N_GROUPS + g] for g in range(N_GROUPS)]
    longest = functools.reduce(jnp.maximum, lens)

    def gather(win):
        acc = jnp.zeros(o_ref.shape, F32)
        for g in range(N_GROUPS):
            w0 = _window_start(bases[g], win)
            slot = lax.broadcasted_iota(jnp.int32, (MOE_SUB, win), 1).astype(F32) + w0.astype(F32)
            ing = jnp.sum(jnp.where(_lane(cb.shape) == N_EXPERTS + g, cb, 0.0), axis=-1, keepdims=True)
            perm_t = jnp.where((slot == dest) & (ing > 0.5), 1.0, 0.0).astype(BF16)
            acc += _dot(perm_t, ys_ref[pl.ds(w0, win), :].astype(BF16))
        o_ref[...] = x_ref[...] + g2_ref[0] * acc

    @pl.when(longest <= UNSORT_WIN_SMALL - 16)
    def _():
        gather(UNSORT_WIN_SMALL)

    @pl.when(longest > UNSORT_WIN_SMALL - 16)
    def _():
        gather(UNSORT_WIN)


def _moe_sort(h2, comb, base, lens):
    nt, d = h2.shape
    n_tiles, n_sub = nt // MOE_TILE, MOE_TILE // MOE_SUB
    return pl.pallas_call(
        _moe_sort_kernel,
        out_shape=[jax.ShapeDtypeStruct((n_tiles * MOE_ROWS, d), BF16),
                   jax.ShapeDtypeStruct((n_tiles * MOE_ROWS, LANES), F32)],
        grid_spec=pltpu.PrefetchScalarGridSpec(
            num_scalar_prefetch=2,
            grid=(n_tiles, n_sub),
            in_specs=[pl.BlockSpec((MOE_SUB, d), lambda i, s, b, n: (i * n_sub + s, 0)),
                      pl.BlockSpec((MOE_SUB, LANES), lambda i, s, b, n: (i * n_sub + s, 0))],
            out_specs=[pl.BlockSpec((MOE_ROWS, d), lambda i, s, b, n: (i, 0), pipeline_mode=_ONCE),
                       pl.BlockSpec((MOE_ROWS, LANES), lambda i, s, b, n: (i, 0), pipeline_mode=_ONCE)],
            scratch_shapes=[pltpu.VMEM((MOE_ROWS, d), F32)]),
        compiler_params=_cparams(("parallel", "arbitrary")),
        name="moe_sort",
    )(base, lens, h2, comb)


def _moe_experts(xs, cs, c0, nc, wg, wu, wd, layer):
    d = xs.shape[1]
    n_tiles = xs.shape[0] // MOE_ROWS
    _, ne, _, ff = wg.shape
    return pl.pallas_call(
        _moe_expert_kernel,
        out_shape=jax.ShapeDtypeStruct((n_tiles * MOE_ROWS, d), F32),
        grid_spec=pltpu.PrefetchScalarGridSpec(
            num_scalar_prefetch=2,
            grid=(n_tiles, ne),
            in_specs=[pl.BlockSpec((MOE_ROWS, d), lambda i, e, a, b: (i, 0), pipeline_mode=_ONCE),
                      pl.BlockSpec((MOE_ROWS, LANES), lambda i, e, a, b: (i, 0), pipeline_mode=_ONCE),
                      pl.BlockSpec((None, 1, d, ff), lambda i, e, a, b: (layer, e, 0, 0)),
                      pl.BlockSpec((None, 1, d, ff), lambda i, e, a, b: (layer, e, 0, 0)),
                      pl.BlockSpec((None, 1, ff, d), lambda i, e, a, b: (layer, e, 0, 0))],
            out_specs=pl.BlockSpec((MOE_ROWS, d), lambda i, e, a, b: (i, 0), pipeline_mode=_ONCE)),
        compiler_params=_cparams(("parallel", "arbitrary")),
        name="moe_experts",
    )(c0, nc, xs, cs, wg, wu, wd)


def _moe_unsort(ys, comb, x, mods3, base, lens, set_of_row, *, row0, n_rows, tn=1024):
    d = x.shape[1]
    n_sub = MOE_TILE // MOE_SUB
    tile0, sub0 = row0 // MOE_TILE, row0 // MOE_SUB
    return pl.pallas_call(
        functools.partial(_moe_unsort_kernel, tile0=tile0),
        out_shape=jax.ShapeDtypeStruct((n_rows, d), F32),
        grid_spec=pltpu.PrefetchScalarGridSpec(
            num_scalar_prefetch=2,
            grid=(n_rows // MOE_TILE, d // tn, n_sub),
            in_specs=[pl.BlockSpec((MOE_ROWS, tn), lambda i, j, s, b, n: (tile0 + i, j)),
                      pl.BlockSpec((MOE_SUB, LANES), lambda i, j, s, b, n: (sub0 + i * n_sub + s, 0)),
                      pl.BlockSpec((MOE_SUB, tn), lambda i, j, s, b, n: (sub0 + i * n_sub + s, j)),
                      pl.BlockSpec((1, 1, tn),
                                   lambda i, j, s, b, n: (set_of_row(row0 + i * MOE_TILE), 0,
                                                          5 * (d // tn) + j))],
            out_specs=pl.BlockSpec((MOE_SUB, tn), lambda i, j, s, b, n: (i * n_sub + s, j))),
        compiler_params=_cparams(("parallel", "parallel", "arbitrary")),
        name="moe_unsort",
    )(base, lens, ys, comb, x, mods3)


def _moe_plan(cnt):
    n_sub = MOE_TILE // MOE_SUB
    c = cnt[:, 0, N_EXPERTS:N_EXPERTS + N_GROUPS].astype(jnp.int32).reshape(-1, n_sub, N_GROUPS)
    seg = (jnp.sum(c, axis=1) + MOE_CHUNK - 1) // MOE_CHUNK * MOE_CHUNK
    seg_start = jnp.cumsum(seg, axis=1) - seg
    base = seg_start[:, None, :] + jnp.cumsum(c, axis=1) - c
    return (base.reshape(-1), c.reshape(-1), (seg_start // MOE_CHUNK).reshape(-1),
            (seg // MOE_CHUNK).reshape(-1))


def _rope_tables(t):
    n_freq = SUB_W // 4
    tok = jnp.arange(t)
    row = (tok // GRID_W).astype(F32)
    col = (tok % GRID_W).astype(F32)
    inv = ROPE_BASE ** (-jnp.arange(n_freq, dtype=F32) / n_freq)
    ar, ac = row[:, None] * inv, col[:, None] * inv
    cos = jnp.concatenate([jnp.cos(ar)] * 2 + [jnp.cos(ac)] * 2, axis=1)
    sin = jnp.concatenate([-jnp.sin(ar), jnp.sin(ar), -jnp.sin(ac), jnp.sin(ac)], axis=1)
    return jnp.tile(cos, (1, 2)), jnp.tile(sin, (1, 2))


def kernel(x_prompt, x_sample, cache_diff_k, cache_diff_v, state_ret, c, c_ctx, w_mod, b_mod,
           g_norm1, g_norm2, w_in, g_qnorm, g_knorm, lambda_qk, g_subln, ret_decay_logit, conv_w,
           conv_b, conv_ln_g, conv_ln_b, w_branch, w_out, w_router_group, b_router_group,
           w_router_expert, b_router_expert, w_exp_gate, w_exp_up, w_exp_down):
    bc, tc, d = x_prompt.shape
    bl, tl, _ = x_sample.shape
    depth = w_mod.shape[0]
    past = cache_diff_k.shape[2]
    n_ctx, n_lat = bc * tc, bl * tl
    assert bl + 1 <= 8 and n_ctx % tl == 0 and n_ctx % MOE_TILE == 0 and tl % MOE_TILE == 0

    def set_of_row(r):
        return jnp.where(r < n_ctx, 0, 1 + (r - n_ctx) // tl)

    tm_big = math.gcd(1024, math.gcd(n_ctx, tl))
    tm_small = math.gcd(512, tm_big)

    x = jnp.concatenate([x_prompt.reshape(n_ctx, d), x_sample.reshape(n_lat, d)], axis=0)
    cond8 = jnp.zeros((8, d), F32).at[0].set(c_ctx).at[1:1 + bl].set(c)
    mods = _mods(cond8, w_mod, b_mod)
    cos, sin = _rope_tables(tl)
    ck = cache_diff_k.reshape(bl, depth, past, N_HEADS * HEAD_W)
    cv = cache_diff_v.reshape(bl, depth, past, N_HEADS * HEAD_W)
    kv_shape = (bc, depth, tc, N_HEADS * HEAD_W)
    st_shape = (bc, depth, 2, N_HEADS, CHUNK, CHUNK)

    w_in_bf, w_br_bf, w_out_bf = w_in.astype(BF16), w_branch.astype(BF16), w_out.astype(BF16)
    wg_bf, wu_bf, wd_bf = w_exp_gate.astype(BF16), w_exp_up.astype(BF16), w_exp_down.astype(BF16)

    new_kv, new_st = None, None
    for l in range(depth):
        lam_init = 0.8 - 0.6 * math.exp(-0.3 * l)
        mods3 = mods[l].reshape(8, 1, 6 * d)
        gq = jnp.tile(g_qnorm[l], 2)[None]
        gk = jnp.tile(g_knorm[l], 2)[None]
        gs = g_subln[l][None]
        decay16 = ret_decay_logit[l].reshape(2 * N_HEADS, 1, 1)

        z = _inproj(x, g_norm1[l][None], mods3, w_in_bf, l, set_of_row, tm=tm_big)

        oa, ak, av = _attention(z, lambda_qk[l], gq, gk, gs, layer=l, row0=0, n_seq=bc, t=tc,
                                tq=tc, lam_init=lam_init, hps=CTX_HEADS_PER_STEP, kv_prev=new_kv,
                                kv_shape=kv_shape)
        new_kv = (ak, av)
        (oa,) = _attention(z, lambda_qk[l], gq, gk, gs, layer=l, row0=n_ctx, n_seq=bl, t=tl,
                           tq=math.gcd(512, tl), lam_init=lam_init, cache=(ck, cv),
                           rope_tabs=(cos, sin), oa_prev=oa)
        orr, new_st = _retention(z, decay16, layer=l, row0=0, n_seq=bc, t=tc, st_prev=new_st,
                                 st_shape=st_shape)
        (orr,) = _retention(z, decay16, layer=l, row0=n_ctx, n_seq=bl, t=tl, s0=state_ret,
                            y_prev=orr)
        cargs = (conv_w[l], conv_b[l][None], conv_ln_g[l][None], conv_ln_b[l][None])
        oc = _conv(z, *cargs, row0=0, n_seq=bc, t=tc, tt=math.gcd(512, tc))
        oc = _conv(z, *cargs, row0=n_ctx, n_seq=bl, t=tl, tt=math.gcd(512, tl), y_prev=oc)

        merged = _merge(oa, orr, oc, z, w_br_bf, l, tm=tm_big)

        w_router = jnp.zeros((d, LANES), F32).at[:, :N_EXPERTS].set(w_router_expert[l])
        w_router = w_router.at[:, N_EXPERTS:N_EXPERTS + N_GROUPS].set(w_router_group[l])
        b_router = jnp.zeros((1, LANES), F32).at[0, :N_EXPERTS].set(b_router_expert[l])
        b_router = b_router.at[0, N_EXPERTS:N_EXPERTS + N_GROUPS].set(b_router_group[l])
        x, h2, comb, cnt = _outproj(merged, x, w_out_bf, l, mods3, g_norm2[l][None],
                                    jnp.concatenate(_split_bf16(w_router, 2), axis=1), b_router,
                                    set_of_row,
                                    tm=tm_small)
        base, lens, c0, nc = _moe_plan(cnt)
        xs, cs = _moe_sort(h2, comb, base, lens)
        ys = _moe_experts(xs, cs, c0, nc, wg_bf, wu_bf, wd_bf, l)
        unsort = functools.partial(_moe_unsort, ys, comb, x, mods3, base, lens, set_of_row)
        if l + 1 < depth:
            x = unsort(row0=0, n_rows=n_ctx + n_lat)
        else:
            yp = unsort(row0=0, n_rows=n_ctx)
            ys_out = unsort(row0=n_ctx, n_rows=n_lat)

    return (yp.reshape(bc, tc, d), ys_out.reshape(bl, tl, d),
            new_kv[0].reshape(bc, depth, tc, N_HEADS, 2, SUB_W),
            new_kv[1].reshape(bc, depth, tc, N_HEADS, HEAD_W), new_st)
```
